```python
import math
import jax, jax.numpy as jnp
from jax import lax
import numpy as np

D_MODEL = 1024
BATCH = 4
SEQ = 4096
DEPTH = 4

N_MIXERS = 3
CHUNK = 64
LN_EPS = 1e-5
RMS_EPS = 1e-6
L2_EPS = 1e-6
DN_ALPHA = (2 * DEPTH) ** 0.25
DN_BETA = (8 * DEPTH) ** -0.25
D_FF = ((8 * D_MODEL + 3 * 256 - 1) // (3 * 256)) * 256

GDN_DK = 128
GDN_DV = 128
GDN_HEADS = D_MODEL // GDN_DK
GDN_KEY = GDN_HEADS * GDN_DK
GDN_VAL = GDN_HEADS * GDN_DV
GDN_CONV = 4
GDN_QKV = 2 * GDN_KEY + GDN_VAL
GDN_IN = GDN_QKV + GDN_VAL + 2 * GDN_HEADS

HGRN_DK = 128
HGRN_HEADS = D_MODEL // HGRN_DK
HGRN_DV = D_MODEL // HGRN_HEADS
HGRN_IN = 4 * D_MODEL

GLA_HEADS = 4
GLA_KEY = D_MODEL // 2
GLA_DK = GLA_KEY // GLA_HEADS
GLA_VAL = D_MODEL
GLA_DV = GLA_VAL // GLA_HEADS
GLA_RANK = 16
GLA_TAU = 16.0
GLA_IN = 2 * GLA_KEY + 2 * GLA_VAL + GLA_RANK

N_GDN = (DEPTH + 2) // 3
N_HGRN = (DEPTH + 1) // 3
N_GLA = DEPTH // 3

kernel_name = "hybrid_gdn_hgrn2_gla_deepnorm"

F32 = jnp.float32


def layer_norm(x, g, b):
    xf = x.astype(F32)
    mu = jnp.mean(xf, -1, keepdims=True)
    var = jnp.mean(jnp.square(xf - mu), -1, keepdims=True)
    return ((xf - mu) * lax.rsqrt(var + LN_EPS) * g + b).astype(x.dtype)


def gated_rms_norm(o, gate, w):
    of = o.astype(F32)
    y = of * lax.rsqrt(jnp.mean(of * of, -1, keepdims=True) + RMS_EPS) * w
    return y * jax.nn.silu(gate.astype(F32))


def l2_normalize(t):
    t = t.astype(F32)
    return t * lax.rsqrt(jnp.sum(t * t, -1, keepdims=True) + L2_EPS)


def causal_depthwise_conv(x, w):
    K, C = w.shape
    return lax.conv_general_dilated(
        x, w[:, None, :].astype(x.dtype), window_strides=(1,), padding=[(K - 1, 0)],
        dimension_numbers=('NWC', 'WIO', 'NWC'), feature_group_count=C)


def to_chunks(t):
    B, T, H = t.shape[:3]
    return jnp.moveaxis(t.reshape(B, T // CHUNK, CHUNK, H, *t.shape[3:]), 3, 1)


def from_chunks(t):
    B, H, N, C, d = t.shape
    return jnp.moveaxis(t, 1, 3).reshape(B, N * C, H, d)


def lead(t):
    return jnp.moveaxis(t, 2, 0)


def chunk_gated_delta_rule(q, k, v, beta, g):
    B, T, H, dk = q.shape
    dv = v.shape[-1]
    q, k, v = to_chunks(q.astype(F32)), to_chunks(k.astype(F32)), to_chunks(v.astype(F32))
    beta, g = to_chunks(beta.astype(F32)), to_chunks(g.astype(F32))
    G = jnp.cumsum(g, axis=-1)
    incl = jnp.tril(jnp.ones((CHUNK, CHUNK), bool))
    strict = jnp.tril(jnp.ones((CHUNK, CHUNK), bool), -1)
    decay = jnp.exp(jnp.where(incl, G[..., :, None] - G[..., None, :], -jnp.inf))
    kb = k * beta[..., None]
    L = jnp.where(strict, jnp.einsum('bhncd,bhnsd->bhncs', kb, k) * decay, 0.0)
    rhs = jnp.concatenate([v * beta[..., None], kb * jnp.exp(G)[..., None]], axis=-1)
    sol = lax.linalg.triangular_solve(L + jnp.eye(CHUNK, dtype=F32), rhs, left_side=True, lower=True)
    u, w = sol[..., :dv], sol[..., dv:]
    a_qk = jnp.einsum('bhncd,bhnsd->bhncs', q, k) * decay
    q_dec = q * jnp.exp(G)[..., None]
    G_last = G[..., -1:]
    k_dec = k * jnp.exp(G_last - G)[..., None]
    g_last = jnp.exp(G_last[..., 0])

    def step(S, inp):
        u_c, w_c, a_c, q_c, k_c, gl_c = inp
        v_new = u_c - jnp.einsum('bhcd,bhde->bhce', w_c, S)
        o_c = jnp.einsum('bhcd,bhde->bhce', q_c, S) + jnp.einsum('bhcs,bhse->bhce', a_c, v_new)
        S = S * gl_c[..., None, None] + jnp.einsum('bhcd,bhce->bhde', k_c, v_new)
        return S, o_c

    S0 = jnp.zeros((B, H, dk, dv), F32)
    _, o = lax.scan(step, S0, (lead(u), lead(w), lead(a_qk), lead(q_dec), lead(k_dec), lead(g_last)))
    return from_chunks(jnp.moveaxis(o, 0, 2))


def chunk_gla(q, k, v, g):
    B, T, H, dk = q.shape
    dv = v.shape[-1]
    q, k, v, g = (to_chunks(t.astype(F32)) for t in (q, k, v, g))
    G = jnp.cumsum(g, axis=-2)
    G_last = G[..., -1:, :]
    q_dec = q * jnp.exp(G)
    k_dec = k * jnp.exp(G_last - G)
    g_last = jnp.exp(G_last)
    incl = jnp.tril(jnp.ones((CHUNK, CHUNK), bool))[..., None]

    def step(S, inp):
        q_c, k_c, v_c, G_c, qd_c, kd_c, gl_c = inp
        rel = jnp.exp(jnp.where(incl, G_c[..., :, None, :] - G_c[..., None, :, :], -jnp.inf))
        a = jnp.einsum('bhcd,bhsd,bhcsd->bhcs', q_c, k_c, rel)
        o_c = jnp.einsum('bhcd,bhde->bhce', qd_c, S) + jnp.einsum('bhcs,bhse->bhce', a, v_c)
        S = S * gl_c[..., 0, :, None] + jnp.einsum('bhcd,bhce->bhde', kd_c, v_c)
        return S, o_c

    S0 = jnp.zeros((B, H, dk, dv), F32)
    xs = (lead(q), lead(k), lead(v), lead(G), lead(q_dec), lead(k_dec), lead(g_last))
    _, o = lax.scan(step, S0, xs)
    return from_chunks(jnp.moveaxis(o, 0, 2))


def gated_deltanet(x, w_in, conv_w, a_log, dt_bias, norm_w, w_out):
    B, T, _ = x.shape
    p = x @ w_in
    qkv, z, b, a = jnp.split(p, [GDN_QKV, GDN_QKV + GDN_VAL, GDN_QKV + GDN_VAL + GDN_HEADS], axis=-1)
    qkv = jax.nn.silu(causal_depthwise_conv(qkv, conv_w))
    q, k, v = jnp.split(qkv, [GDN_KEY, 2 * GDN_KEY], axis=-1)
    q = l2_normalize(q.reshape(B, T, GDN_HEADS, GDN_DK)) * (GDN_DK ** -0.5)
    k = l2_normalize(k.reshape(B, T, GDN_HEADS, GDN_DK))
    v = v.reshape(B, T, GDN_HEADS, GDN_DV)
    beta = jax.nn.sigmoid(b.astype(F32))
    g = -jnp.exp(a_log.astype(F32)) * jax.nn.softplus(a.astype(F32) + dt_bias.astype(F32))
    o = chunk_gated_delta_rule(q, k, v, beta, g)
    o = gated_rms_norm(o, z.reshape(B, T, GDN_HEADS, GDN_DV), norm_w.astype(F32))
    return o.reshape(B, T, GDN_VAL).astype(x.dtype) @ w_out


def hgrn2(x, w_in, lb, norm_w, w_out):
    B, T, _ = x.shape
    q, f, i, r = jnp.split(x @ w_in, 4, axis=-1)
    q = jax.nn.silu(q.astype(F32)).reshape(B, T, HGRN_HEADS, HGRN_DK) * (HGRN_DK ** -0.5)
    log_f = jnp.logaddexp(jnp.log(lb), jnp.log1p(-lb) + jax.nn.log_sigmoid(f.astype(F32)))
    k = -jnp.expm1(log_f)
    shp_k = (B, T, HGRN_HEADS, HGRN_DK)
    o = chunk_gla(q, k.reshape(shp_k), i.reshape(B, T, HGRN_HEADS, HGRN_DV), log_f.reshape(shp_k))
    o = gated_rms_norm(o, r.reshape(B, T, HGRN_HEADS, HGRN_DV), norm_w.astype(F32))
    return o.reshape(B, T, D_MODEL).astype(x.dtype) @ w_out


def gla(x, w_in, w_gk2, b_gk, norm_w, w_out):
    B, T, _ = x.shape
    q, k, v, r, gk = jnp.split(x @ w_in, [GLA_KEY, 2 * GLA_KEY, 2 * GLA_KEY + GLA_VAL, 2 * GLA_KEY + 2 * GLA_VAL], axis=-1)
    g = jax.nn.log_sigmoid((gk @ w_gk2 + b_gk).astype(F32)) / GLA_TAU
    shp_k = (B, T, GLA_HEADS, GLA_DK)
    q = q.astype(F32).reshape(shp_k) * (GLA_DK ** -0.5)
    o = chunk_gla(q, k.reshape(shp_k), v.reshape(B, T, GLA_HEADS, GLA_DV), g.reshape(shp_k))
    o = gated_rms_norm(o, r.reshape(B, T, GLA_HEADS, GLA_DV), norm_w.astype(F32))
    return o.reshape(B, T, GLA_VAL).astype(x.dtype) @ w_out


def swiglu(x, w_in, w_down):
    gate, up = jnp.split(x @ w_in, 2, axis=-1)
    return (jax.nn.silu(gate) * up) @ w_down


def setup_inputs(seed: int = 0) -> dict:
    key = jax.random.key(seed)
    ks = jax.random.split(key, 22)
    nrm = lambda k, shape, s: jax.random.normal(k, shape, F32) * s
    x = jax.random.normal(ks[0], (BATCH, SEQ, D_MODEL), F32)
    gdn_w_in = nrm(ks[1], (N_GDN, D_MODEL, GDN_IN), D_MODEL ** -0.5)
    gdn_conv_w = nrm(ks[2], (N_GDN, GDN_CONV, GDN_QKV), GDN_CONV ** -0.5)
    gdn_a_log = jnp.log(jax.random.uniform(ks[3], (N_GDN, GDN_HEADS), F32, 1.0, 16.0))
    dt = jnp.exp(jax.random.uniform(ks[4], (N_GDN, GDN_HEADS), F32, math.log(1e-3), math.log(1e-1)))
    gdn_dt_bias = dt + jnp.log(-jnp.expm1(-dt))
    gdn_norm_w = 1.0 + nrm(ks[5], (N_GDN, GDN_DV), 0.02)
    gdn_w_out = nrm(ks[6], (N_GDN, GDN_VAL, D_MODEL), GDN_VAL ** -0.5 * DN_BETA)
    hgrn_w_in = nrm(ks[7], (N_HGRN, D_MODEL, HGRN_IN), D_MODEL ** -0.5)
    hgrn_lb_logits = nrm(ks[8], (DEPTH, HGRN_HEADS * HGRN_DK), 0.5)
    hgrn_norm_w = 1.0 + nrm(ks[9], (N_HGRN, HGRN_DV), 0.02)
    hgrn_w_out = nrm(ks[10], (N_HGRN, D_MODEL, D_MODEL), D_MODEL ** -0.5 * DN_BETA)
    gla_w_in = nrm(ks[11], (N_GLA, D_MODEL, GLA_IN), D_MODEL ** -0.5)
    gla_w_gk2 = nrm(ks[12], (N_GLA, GLA_RANK, GLA_KEY), GLA_RANK ** -0.5)
    gla_b_gk = nrm(ks[13], (N_GLA, GLA_KEY), 0.1)
    gla_norm_w = 1.0 + nrm(ks[14], (N_GLA, GLA_DV), 0.02)
    gla_w_out = nrm(ks[15], (N_GLA, GLA_VAL, D_MODEL), GLA_VAL ** -0.5 * DN_BETA)
    mix_ln_g = 1.0 + nrm(ks[16], (DEPTH, D_MODEL), 0.02)
    mix_ln_b = nrm(ks[17], (DEPTH, D_MODEL), 0.02)
    ffn_w_in = nrm(ks[18], (DEPTH, D_MODEL, 2 * D_FF), D_MODEL ** -0.5)
    ffn_w_down = nrm(ks[19], (DEPTH, D_FF, D_MODEL), D_FF ** -0.5 * DN_BETA)
    ffn_ln_g = 1.0 + nrm(ks[20], (DEPTH, D_MODEL), 0.02)
    ffn_ln_b = nrm(ks[21], (DEPTH, D_MODEL), 0.02)
    return {"x": x, "gdn_w_in": gdn_w_in, "gdn_conv_w": gdn_conv_w, "gdn_a_log": gdn_a_log,
            "gdn_dt_bias": gdn_dt_bias, "gdn_norm_w": gdn_norm_w, "gdn_w_out": gdn_w_out,
            "hgrn_w_in": hgrn_w_in, "hgrn_lb_logits": hgrn_lb_logits, "hgrn_norm_w": hgrn_norm_w,
            "hgrn_w_out": hgrn_w_out, "gla_w_in": gla_w_in, "gla_w_gk2": gla_w_gk2, "gla_b_gk": gla_b_gk,
            "gla_norm_w": gla_norm_w, "gla_w_out": gla_w_out, "mix_ln_g": mix_ln_g, "mix_ln_b": mix_ln_b,
            "ffn_w_in": ffn_w_in, "ffn_w_down": ffn_w_down, "ffn_ln_g": ffn_ln_g, "ffn_ln_b": ffn_ln_b}


def reference(x, gdn_w_in, gdn_conv_w, gdn_a_log, gdn_dt_bias, gdn_norm_w, gdn_w_out,
              hgrn_w_in, hgrn_lb_logits, hgrn_norm_w, hgrn_w_out,
              gla_w_in, gla_w_gk2, gla_b_gk, gla_norm_w, gla_w_out,
              mix_ln_g, mix_ln_b, ffn_w_in, ffn_w_down, ffn_ln_g, ffn_ln_b):
    lb_soft = jax.nn.softmax(hgrn_lb_logits.astype(F32), axis=0)
    lb_all = jnp.cumsum(lb_soft, axis=0) - lb_soft[0]
    for i in range(DEPTH):
        kind, j = i % N_MIXERS, i // N_MIXERS
        if kind == 0:
            m = gated_deltanet(x, gdn_w_in[j], gdn_conv_w[j], gdn_a_log[j], gdn_dt_bias[j], gdn_norm_w[j], gdn_w_out[j])
        elif kind == 1:
            m = hgrn2(x, hgrn_w_in[j], lb_all[i], hgrn_norm_w[j], hgrn_w_out[j])
        else:
            m = gla(x, gla_w_in[j], gla_w_gk2[j], gla_b_gk[j], gla_norm_w[j], gla_w_out[j])
        x = layer_norm(DN_ALPHA * x + m, mix_ln_g[i], mix_ln_b[i])
        x = layer_norm(DN_ALPHA * x + swiglu(x, ffn_w_in[i], ffn_w_down[i]), ffn_ln_g[i], ffn_ln_b[i])
    return x
```

```python
import functools
import math

import jax
import jax.numpy as jnp
from jax import lax
from jax.experimental import pallas as pl
from jax.experimental.pallas import tpu as pltpu

F32 = jnp.float32
BF16 = jnp.bfloat16
HI = lax.Precision.HIGHEST

D_MODEL = 1024
DEPTH = 4
CHUNK = 64
SUB = 16
SUB_SHIFT = 4
LN_EPS = 1e-5
RMS_EPS = 1e-6
L2_EPS = 1e-6
DN_ALPHA = (2 * DEPTH) ** 0.25
D_FF = 2816
LANE = 128
GDN_HEADS = 8
GLA_TAU = 16.0

VMEM_LIMIT = 56 * 1024 * 1024


def _cparams(sem):
    return pltpu.CompilerParams(dimension_semantics=sem, vmem_limit_bytes=VMEM_LIMIT)


def _dot(a, b, precision=None):
    return jnp.dot(a, b, preferred_element_type=F32, precision=precision)


def _dot_nt(a, b, precision=None):
    return lax.dot_general(a, b, (((1,), (1,)), ((), ())), preferred_element_type=F32, precision=precision)


def _sigmoid(x):
    return 1.0 / (1.0 + jnp.exp(-x))


def _silu(x):
    return x * _sigmoid(x)


def _log_sigmoid(x):
    return jnp.minimum(x, 0.0) - jnp.log1p(jnp.exp(-jnp.abs(x)))


def _softplus(x):
    return jnp.maximum(x, 0.0) + jnp.log1p(jnp.exp(-jnp.abs(x)))


def _layer_norm(h, g, b):
    mu = jnp.mean(h, axis=-1, keepdims=True)
    d = h - mu
    var = jnp.mean(d * d, axis=-1, keepdims=True)
    return d * lax.rsqrt(var + LN_EPS) * g + b


def _iota2(shape, dim):
    return lax.broadcasted_iota(jnp.int32, shape, dim)


def _tril_ones(n):
    return (_iota2((n, n), 0) >= _iota2((n, n), 1)).astype(F32)


def _inproj_kernel(x_ref, w_ref, ws_ref, y_ref, ys_ref, xb_ref):
    j = pl.program_id(1)

    @pl.when(j == 0)
    def _():
        xb = x_ref[...].astype(BF16)
        xb_ref[...] = xb
        ys_ref[...] = _dot(xb, ws_ref[...])

    y_ref[...] = _dot(xb_ref[...], w_ref[...])


def _inproj(x2, w, ws, tm=1024, tn=512):
    n, k = x2.shape
    m = w.shape[1]
    return pl.pallas_call(
        _inproj_kernel,
        grid=(n // tm, m // tn),
        in_specs=[pl.BlockSpec((tm, k), lambda i, j: (i, 0)),
                  pl.BlockSpec((k, tn), lambda i, j: (0, j)),
                  pl.BlockSpec((k, LANE), lambda i, j: (0, 0))],
        out_specs=[pl.BlockSpec((tm, tn), lambda i, j: (i, j)),
                   pl.BlockSpec((tm, LANE), lambda i, j: (i, 0))],
        out_shape=[jax.ShapeDtypeStruct((n, m), F32), jax.ShapeDtypeStruct((n, LANE), F32)],
        scratch_shapes=[pltpu.VMEM((tm, k), BF16)],
        compiler_params=_cparams(("parallel", "arbitrary")),
        name="inproj",
    )(x2, w, ws)


def _outproj_ln_kernel(o_ref, w_ref, x_ref, g_ref, b_ref, y_ref):
    m = _dot(o_ref[...], w_ref[...])
    y_ref[...] = _layer_norm(DN_ALPHA * x_ref[...] + m, g_ref[...], b_ref[...])


def _outproj_ln(o2, w, x2, g, b, tm=512):
    n, k = o2.shape
    d = w.shape[1]
    return pl.pallas_call(
        _outproj_ln_kernel,
        grid=(n // tm,),
        in_specs=[pl.BlockSpec((tm, k), lambda i: (i, 0)),
                  pl.BlockSpec((k, d), lambda i: (0, 0)),
                  pl.BlockSpec((tm, d), lambda i: (i, 0)),
                  pl.BlockSpec((1, d), lambda i: (0, 0)),
                  pl.BlockSpec((1, d), lambda i: (0, 0))],
        out_specs=pl.BlockSpec((tm, d), lambda i: (i, 0)),
        out_shape=jax.ShapeDtypeStruct((n, d), F32),
        compiler_params=_cparams(("parallel",)),
        name="outproj_ln",
    )(o2, w, x2, g, b)


def _ffn_kernel(x_ref, wg_ref, wu_ref, wd_ref, g_ref, b_ref, y_ref, xb_ref, acc_ref):
    f = pl.program_id(1)

    @pl.when(f == 0)
    def _():
        xb_ref[...] = x_ref[...].astype(BF16)
        acc_ref[...] = jnp.zeros_like(acc_ref)

    xb = xb_ref[...]
    gate = _dot(xb, wg_ref[...])
    up = _dot(xb, wu_ref[...])
    h = (_silu(gate) * up).astype(BF16)
    acc_ref[...] += _dot(h, wd_ref[...])

    @pl.when(f == pl.num_programs(1) - 1)
    def _():
        y_ref[...] = _layer_norm(DN_ALPHA * x_ref[...] + acc_ref[...], g_ref[...], b_ref[...])


def _ffn(x2, w_in, w_down, g, b, tm=1024, tf=256):
    n, d = x2.shape
    nf = D_FF // tf
    return pl.pallas_call(
        _ffn_kernel,
        grid=(n // tm, nf),
        in_specs=[pl.BlockSpec((tm, d), lambda i, f: (i, 0)),
                  pl.BlockSpec((d, tf), lambda i, f: (0, f)),
                  pl.BlockSpec((d, tf), lambda i, f: (0, f + nf)),
                  pl.BlockSpec((tf, d), lambda i, f: (f, 0)),
                  pl.BlockSpec((1, d), lambda i, f: (0, 0)),
                  pl.BlockSpec((1, d), lambda i, f: (0, 0))],
        out_specs=pl.BlockSpec((tm, d), lambda i, f: (i, 0)),
        out_shape=jax.ShapeDtypeStruct((n, d), F32),
        scratch_shapes=[pltpu.VMEM((tm, d), BF16), pltpu.VMEM((tm, d), F32)],
        compiler_params=_cparams(("parallel", "arbitrary")),
        name="ffn",
    )(x2, w_in, w_in, w_down, g, b)


def _gated_rms_norm(o, gate, w):
    y = o * lax.rsqrt(jnp.mean(o * o, axis=-1, keepdims=True) + RMS_EPS) * w
    return y * _silu(gate)


def _unit_lower_inverse(l_strict):
    n = CHUNK
    nb = n // SUB
    row = _iota2((n, n), 0)
    col = _iota2((n, n), 1)
    rb = row >> SUB_SHIFT
    cb = col >> SUB_SHIFT
    x = (row == col).astype(F32).reshape(nb, SUB, n)
    ld = jnp.where(rb == cb, l_strict, 0.0)
    for j in range(SUB - 1):
        colsel = (col & (SUB - 1)) == j
        cj = jnp.sum(jnp.where(colsel, ld, 0.0), axis=-1, keepdims=True)
        piv = x[:, j:j + 1, :]
        x = x - cj.reshape(nb, SUB, 1) * piv
    x = x.reshape(n, n)
    shift = SUB_SHIFT
    width = SUB
    while width < n:
        sel = ((row >> (shift + 1)) == (col >> (shift + 1))) & ((row >> shift) != (col >> shift))
        shift += 1
        lo = jnp.where(sel, l_strict, 0.0)
        x = x - _dot(x, _dot(lo, x, HI), HI)
        width *= 2
    return x


def _gdn_kernel(q_ref, k_ref, v_ref, z_ref, ba_ref, cwq_ref, cwk_ref, cwv_ref, par_ref, nw_ref,
                o_ref,
                s_ref, xq_ref, xk_ref, xv_ref, qs_ref, ks_ref, vs_ref, g_ref, beta_ref,
                u_ref, w_ref, a_ref, qd_ref, kdt_ref, gl_ref, *, hb, tc):
    hblk = pl.program_id(1)
    t = pl.program_id(2)
    dk = LANE
    nchunk = tc // CHUNK

    @pl.when(t == 0)
    def _():
        s_ref[...] = jnp.zeros_like(s_ref)
        xq_ref[0:8, :] = jnp.zeros((8, hb * dk), F32)
        xk_ref[0:8, :] = jnp.zeros((8, hb * dk), F32)
        xv_ref[0:8, :] = jnp.zeros((8, hb * dk), F32)

    def conv_silu(src_ref, xs_ref, cw_ref):
        xs_ref[8:8 + tc, :] = src_ref[0]
        acc = cw_ref[3:4, :] * xs_ref[8:8 + tc, :]
        for j in range(3):
            acc = acc + cw_ref[j:j + 1, :] * xs_ref[5 + j:5 + j + tc, :]
        xs_ref[0:8, :] = xs_ref[tc:tc + 8, :]
        return _silu(acc)

    qc = conv_silu(q_ref, xq_ref, cwq_ref)
    kc = conv_silu(k_ref, xk_ref, cwk_ref)
    vs_ref[...] = conv_silu(v_ref, xv_ref, cwv_ref)
    for hh in range(hb):
        sl = slice(hh * dk, (hh + 1) * dk)
        qh = qc[:, sl]
        kh = kc[:, sl]
        qs_ref[:, sl] = qh * (lax.rsqrt(jnp.sum(qh * qh, axis=-1, keepdims=True) + L2_EPS) * (dk ** -0.5))
        ks_ref[:, sl] = kh * lax.rsqrt(jnp.sum(kh * kh, axis=-1, keepdims=True) + L2_EPS)

    ba = ba_ref[0]
    beta_ref[...] = _sigmoid(ba)
    g_ref[...] = -jnp.exp(par_ref[0:1, :]) * _softplus(ba + par_ref[1:2, :])

    tri = _tril_ones(CHUNK)
    row = _iota2((CHUNK, CHUNK), 0)
    col = _iota2((CHUNK, CHUNK), 1)
    incl = row >= col
    strict = row > col
    lane = _iota2((CHUNK, LANE), 1)
    subl = _iota2((LANE, CHUNK), 0)

    def prep(c, carry):
        r0 = pl.multiple_of(c * CHUNK, CHUNK)
        g_all = _dot(tri, g_ref[pl.ds(r0, CHUNK), :], HI)
        g_all_t = g_all.T
        b_all = beta_ref[pl.ds(r0, CHUNK), :]
        for hh in range(hb):
            sl = slice(hh * dk, (hh + 1) * dk)
            h = hblk * hb + hh
            gc = jnp.sum(jnp.where(lane == GDN_HEADS + h, g_all, 0.0), axis=-1, keepdims=True)
            beta = jnp.sum(jnp.where(lane == h, b_all, 0.0), axis=-1, keepdims=True)
            gr = jnp.sum(jnp.where(subl == GDN_HEADS + h, g_all_t, 0.0), axis=0, keepdims=True)
            q = qs_ref[pl.ds(r0, CHUNK), sl]
            k = ks_ref[pl.ds(r0, CHUNK), sl]
            v = vs_ref[pl.ds(r0, CHUNK), sl]
            decay = jnp.exp(jnp.where(incl, gc - gr, -jnp.inf))
            kb16 = k.astype(BF16)
            kk = _dot_nt(kb16, kb16)
            qk = _dot_nt(q.astype(BF16), kb16)
            l_strict = jnp.where(strict, kk * beta * decay, 0.0)
            tinv = _unit_lower_inverse(l_strict).astype(BF16)
            eg = jnp.exp(gc)
            u_ref[pl.ds(r0, CHUNK), sl] = _dot(tinv, (v * beta).astype(BF16))
            w_ref[pl.ds(r0, CHUNK), sl] = _dot(tinv, (k * (beta * eg)).astype(BF16)).astype(BF16)
            a_ref[hh, c] = (qk * decay).astype(BF16)
            g_last = gc[CHUNK - 1:CHUNK, :]
            qd_ref[pl.ds(r0, CHUNK), sl] = (q * eg).astype(BF16)
            kdt_ref[hh, c] = (k * jnp.exp(g_last - gc)).T.astype(BF16)
            gl_ref[hh, pl.ds(pl.multiple_of(c * 8, 8), 8), :] = jnp.broadcast_to(jnp.exp(g_last), (8, LANE))
        return carry

    lax.fori_loop(0, nchunk, prep, 0)

    def scan(c, carry):
        r0 = pl.multiple_of(c * CHUNK, CHUNK)
        for hh in range(hb):
            sl = slice(hh * dk, (hh + 1) * dk)
            s = s_ref[hh]
            s16 = s.astype(BF16)
            v_new = u_ref[pl.ds(r0, CHUNK), sl] - _dot(w_ref[pl.ds(r0, CHUNK), sl], s16)
            vn16 = v_new.astype(BF16)
            o = _dot(qd_ref[pl.ds(r0, CHUNK), sl], s16) + _dot(a_ref[hh, c], vn16)
            gl = gl_ref[hh, pl.ds(pl.multiple_of(c * 8, 8), 1), :]
            s_ref[hh] = s * gl + _dot(kdt_ref[hh, c], vn16)
            y = _gated_rms_norm(o, z_ref[0, pl.ds(r0, CHUNK), sl], nw_ref[...])
            o_ref[0, pl.ds(r0, CHUNK), sl] = y.astype(o_ref.dtype)
        return carry

    lax.fori_loop(0, nchunk, scan, 0)


def _gdn_mixer(p, ba, conv_w, par, norm_w, batch, seq, hb=2, tc=256):
    dk = LANE
    w = hb * dk
    nhb = GDN_HEADS // hb
    nchunk = tc // CHUNK
    col = lambda off: (lambda b, h, t: (b, t, off * nhb + h))
    cwcol = lambda off: (lambda b, h, t: (0, off * nhb + h))
    kern = functools.partial(_gdn_kernel, hb=hb, tc=tc)
    return pl.pallas_call(
        kern,
        grid=(batch, nhb, seq // tc),
        in_specs=[pl.BlockSpec((1, tc, w), col(0)),
                  pl.BlockSpec((1, tc, w), col(1)),
                  pl.BlockSpec((1, tc, w), col(2)),
                  pl.BlockSpec((1, tc, w), col(3)),
                  pl.BlockSpec((1, tc, LANE), lambda b, h, t: (b, t, 0)),
                  pl.BlockSpec((4, w), cwcol(0)),
                  pl.BlockSpec((4, w), cwcol(1)),
                  pl.BlockSpec((4, w), cwcol(2)),
                  pl.BlockSpec((2, LANE), lambda b, h, t: (0, 0)),
                  pl.BlockSpec((1, dk), lambda b, h, t: (0, 0))],
        out_specs=pl.BlockSpec((1, tc, w), lambda b, h, t: (b, t, h)),
        out_shape=jax.ShapeDtypeStruct((batch, seq, GDN_HEADS * dk), BF16),
        scratch_shapes=[pltpu.VMEM((hb, dk, dk), F32),
                        pltpu.VMEM((tc + 8, w), F32),
                        pltpu.VMEM((tc + 8, w), F32),
                        pltpu.VMEM((tc + 8, w), F32),
                        pltpu.VMEM((tc, w), F32),
                        pltpu.VMEM((tc, w), F32),
                        pltpu.VMEM((tc, w), F32),
                        pltpu.VMEM((tc, LANE), F32),
                        pltpu.VMEM((tc, LANE), F32),
                        pltpu.VMEM((tc, w), F32),
                        pltpu.VMEM((tc, w), BF16),
                        pltpu.VMEM((hb, nchunk, CHUNK, CHUNK), BF16),
                        pltpu.VMEM((tc, w), BF16),
                        pltpu.VMEM((hb, nchunk, dk, CHUNK), BF16),
                        pltpu.VMEM((hb, nchunk * 8, LANE), F32)],
        compiler_params=_cparams(("parallel", "parallel", "arbitrary")),
        name="gdn_recurrence",
    )(p, p, p, p, ba, conv_w, conv_w, conv_w, par, norm_w)


def _gla_chunk(q, k, v, g, s_t):
    c, dk = q.shape
    big = _dot(_tril_ones(c), g, HI)
    g_last = big[c - 1:c, :]
    q_dec = q * jnp.exp(big)
    k_dec = k * jnp.exp(g_last - big)
    rowk = _iota2((c, dk), 0)

    pieces = [jnp.zeros((SUB, c), F32)]
    for i in range(1, c // SUB):
        g_ref_row = big[i * SUB - 1:i * SUB, :]
        qi = q[i * SUB:(i + 1) * SUB, :] * jnp.exp(big[i * SUB:(i + 1) * SUB, :] - g_ref_row)
        kj = k * jnp.exp(jnp.where(rowk < i * SUB, g_ref_row - big, -jnp.inf))
        pieces.append(_dot_nt(qi.astype(BF16), kj.astype(BF16)))
    a = jnp.concatenate(pieces, axis=0)

    row = _iota2((c, c), 0)
    col = _iota2((c, c), 1)
    rloc = _iota2((c, dk), 0) & (SUB - 1)
    for d in range(SUB):
        if d == 0:
            kd, gd = k, big
        else:
            kd = pltpu.roll(k, d, 0)
            gd = pltpu.roll(big, d, 0)
        e = jnp.exp(jnp.where(rloc >= d, big - gd, -jnp.inf))
        diag = jnp.sum(q * kd * e, axis=-1, keepdims=True)
        a = a + jnp.where(col == row - d, diag, 0.0)

    s16 = s_t.astype(BF16)
    v16 = v.astype(BF16)
    o = _dot_nt(q_dec.astype(BF16), s16) + _dot(a.astype(BF16), v16)
    s_new = s_t * jnp.exp(g_last) + _dot(v.T.astype(BF16), k_dec.astype(BF16))
    return o, s_new


def _gla_kernel(q_ref, k_ref, v_ref, r_ref, aux_ref, w2_ref, b2_ref, nw_ref,
                o_ref,
                s_ref, qs_ref, ks_ref, g_ref, *, hb, tc, dv, mode, layer):
    t = pl.program_id(2)
    dk = LANE
    nchunk = tc // CHUNK

    @pl.when(t == 0)
    def _():
        s_ref[...] = jnp.zeros_like(s_ref)

    if mode == "hgrn":
        lg = aux_ref[...]
        e = jnp.exp(lg - jnp.max(lg, axis=0, keepdims=True))
        soft = e / jnp.sum(e, axis=0, keepdims=True)
        lb = jnp.zeros((1, hb * dk), F32)
        for j in range(1, layer + 1):
            lb = lb + soft[j:j + 1, :]
        f_raw = k_ref[0]
        bterm = jnp.log1p(-lb) + _log_sigmoid(f_raw)
        log_lb = jnp.log(lb)
        mx = jnp.maximum(log_lb, bterm)
        g_ref[...] = mx + jnp.log1p(jnp.exp(-jnp.abs(log_lb - bterm)))
        ks_ref[...] = (1.0 - lb) * _sigmoid(-f_raw)
        qs_ref[...] = _silu(q_ref[0]) * (dk ** -0.5)
    else:
        x = _dot(aux_ref[0], w2_ref[...], HI) + b2_ref[...]
        g_ref[...] = _log_sigmoid(x) / GLA_TAU
        ks_ref[...] = k_ref[0]
        qs_ref[...] = q_ref[0] * (dk ** -0.5)

    def body(c, carry):
        r0 = pl.multiple_of(c * CHUNK, CHUNK)
        for hh in range(hb):
            sl = slice(hh * dk, (hh + 1) * dk)
            slv = slice(hh * dv, (hh + 1) * dv)
            o, s_new = _gla_chunk(qs_ref[pl.ds(r0, CHUNK), sl], ks_ref[pl.ds(r0, CHUNK), sl],
                                  v_ref[0, pl.ds(r0, CHUNK), slv], g_ref[pl.ds(r0, CHUNK), sl], s_ref[hh])
            s_ref[hh] = s_new
            y = _gated_rms_norm(o, r_ref[0, pl.ds(r0, CHUNK), slv], nw_ref[...])
            o_ref[0, pl.ds(r0, CHUNK), slv] = y.astype(o_ref.dtype)
        return carry

    lax.fori_loop(0, nchunk, body, 0)


def _gla_mixer(p, aux, w2, b2, norm_w, batch, seq, *, mode, layer, heads, dv, cols, hb=2, tc=256):
    dk = LANE
    wk = hb * dk
    wv = hb * dv
    nhb = heads // hb
    qo, ko, vo, ro = cols
    kern = functools.partial(_gla_kernel, hb=hb, tc=tc, dv=dv, mode=mode, layer=layer)
    if mode == "hgrn":
        aux_spec = pl.BlockSpec((DEPTH, wk), lambda b, h, t: (0, h))
    else:
        aux_spec = pl.BlockSpec((1, tc, LANE), lambda b, h, t: (b, t, 0))
    return pl.pallas_call(
        kern,
        grid=(batch, nhb, seq // tc),
        in_specs=[pl.BlockSpec((1, tc, wk), lambda b, h, t: (b, t, qo * nhb + h)),
                  pl.BlockSpec((1, tc, wk), lambda b, h, t: (b, t, ko * nhb + h)),
                  pl.BlockSpec((1, tc, wv), lambda b, h, t: (b, t, vo * nhb + h)),
                  pl.BlockSpec((1, tc, wv), lambda b, h, t: (b, t, ro * nhb + h)),
                  aux_spec,
                  pl.BlockSpec((LANE, wk), lambda b, h, t: (0, h)),
                  pl.BlockSpec((1, wk), lambda b, h, t: (0, h)),
                  pl.BlockSpec((1, dv), lambda b, h, t: (0, 0))],
        out_specs=pl.BlockSpec((1, tc, wv), lambda b, h, t: (b, t, h)),
        out_shape=jax.ShapeDtypeStruct((batch, seq, heads * dv), BF16),
        scratch_shapes=[pltpu.VMEM((hb, dv, dk), F32),
                        pltpu.VMEM((tc, wk), F32),
                        pltpu.VMEM((tc, wk), F32),
                        pltpu.VMEM((tc, wk), F32)],
        compiler_params=_cparams(("parallel", "parallel", "arbitrary")),
        name=mode + "_recurrence",
    )(p, p, p, p, aux, w2, b2, norm_w)


def _pad_cols(w, width=LANE):
    return jnp.pad(w, ((0, 0), (0, width - w.shape[1])))


def kernel(x, gdn_w_in, gdn_conv_w, gdn_a_log, gdn_dt_bias, gdn_norm_w, gdn_w_out, hgrn_w_in, hgrn_lb_logits, hgrn_norm_w, hgrn_w_out, gla_w_in, gla_w_gk2, gla_b_gk, gla_norm_w, gla_w_out, mix_ln_g, mix_ln_b, ffn_w_in, ffn_w_down, ffn_ln_g, ffn_ln_b):
    batch, seq, d = x.shape
    n = batch * seq
    x2 = x.reshape(n, d)
    zero_side = jnp.zeros((d, LANE), BF16)
    for i in range(DEPTH):
        kind, j = i % 3, i // 3
        if kind == 0:
            w_in = gdn_w_in[j]
            p, ba = _inproj(x2, w_in[:, :4096].astype(BF16), _pad_cols(w_in[:, 4096:]).astype(BF16))
            par = jnp.zeros((2, LANE), F32)
            par = par.at[0, GDN_HEADS:2 * GDN_HEADS].set(gdn_a_log[j]).at[1, GDN_HEADS:2 * GDN_HEADS].set(gdn_dt_bias[j])
            o = _gdn_mixer(p.reshape(batch, seq, 4096), ba.reshape(batch, seq, LANE), gdn_conv_w[j], par,
                           gdn_norm_w[j].reshape(1, -1), batch, seq)
            w_out = gdn_w_out[j]
        elif kind == 1:
            p, _ = _inproj(x2, hgrn_w_in[j].astype(BF16), zero_side)
            o = _gla_mixer(p.reshape(batch, seq, 4096), hgrn_lb_logits, jnp.zeros((LANE, D_MODEL), F32),
                           jnp.zeros((1, D_MODEL), F32), hgrn_norm_w[j].reshape(1, -1), batch, seq,
                           mode="hgrn", layer=i, heads=8, dv=128, cols=(0, 1, 2, 3))
            w_out = hgrn_w_out[j]
        else:
            w_in = gla_w_in[j]
            p, gk = _inproj(x2, w_in[:, :3072].astype(BF16), _pad_cols(w_in[:, 3072:]).astype(BF16))
            w2 = jnp.pad(gla_w_gk2[j], ((0, LANE - gla_w_gk2.shape[1]), (0, 0)))
            o = _gla_mixer(p.reshape(batch, seq, 3072), gk.reshape(batch, seq, LANE), w2,
                           gla_b_gk[j].reshape(1, -1), gla_norm_w[j].reshape(1, -1), batch, seq,
                           mode="gla", layer=i, heads=4, dv=256, cols=(0, 1, 1, 2))
            w_out = gla_w_out[j]
        x2 = _outproj_ln(o.reshape(n, -1), w_out.astype(BF16), x2, mix_ln_g[i].reshape(1, -1), mix_ln_b[i].reshape(1, -1))
        x2 = _ffn(x2, ffn_w_in[i].astype(BF16), ffn_w_down[i].astype(BF16), ffn_ln_g[i].reshape(1, -1), ffn_ln_b[i].reshape(1, -1))
    return x2.reshape(batch, seq, d)
```

```python
import functools
import math

import jax
import jax.numpy as jnp
from jax import lax
from jax.experimental import pallas as pl
from jax.experimental.pallas import tpu as pltpu

F32 = jnp.float32
BF16 = jnp.bfloat16
HI = lax.Precision.HIGHEST

D_MODEL = 1024
DEPTH = 4
CHUNK = 64
CHUNK_SHIFT = 6
SUB = 4
GSUB = 8
GSUB_SHIFT = 3
LN_EPS = 1e-5
RMS_EPS = 1e-6
L2_EPS = 1e-6
DN_ALPHA = (2 * DEPTH) ** 0.25
D_FF = 2816
LANE = 128
GDN_HEADS = 8
GLA_TAU = 16.0

VMEM_LIMIT = 56 * 1024 * 1024


def _cparams(sem):
    return pltpu.CompilerParams(dimension_semantics=sem, vmem_limit_bytes=VMEM_LIMIT)


def _dot(a, b, precision=None):
    return jnp.dot(a, b, preferred_element_type=F32, precision=precision)


def _dot_nt(a, b, precision=None):
    return lax.dot_general(a, b, (((1,), (1,)), ((), ())), preferred_element_type=F32, precision=precision)


def _sigmoid(x):
    return 1.0 / (1.0 + jnp.exp(-x))


def _silu(x):
    return x * _sigmoid(x)


def _log1p_unit(x):
    return jnp.log(1.0 + x)


def _log_sigmoid(x):
    return jnp.minimum(x, 0.0) - _log1p_unit(jnp.exp(-jnp.abs(x)))


def _softplus(x):
    return jnp.maximum(x, 0.0) + _log1p_unit(jnp.exp(-jnp.abs(x)))


def _chunk_cumsum(g, tc):
    rt = _iota2((tc, tc), 0)
    ct = _iota2((tc, tc), 1)
    tri = ((rt >= ct) & ((rt >> CHUNK_SHIFT) == (ct >> CHUNK_SHIFT))).astype(BF16)
    hi = g.astype(BF16)
    r1 = g - hi.astype(F32)
    mid = r1.astype(BF16)
    lo = (r1 - mid.astype(F32)).astype(BF16)
    return _dot(tri, hi) + _dot(tri, mid) + _dot(tri, lo)


def _layer_norm(h, g, b):
    mu = jnp.mean(h, axis=-1, keepdims=True)
    d = h - mu
    var = jnp.mean(d * d, axis=-1, keepdims=True)
    return d * lax.rsqrt(var + LN_EPS) * g + b


def _iota2(shape, dim):
    return lax.broadcasted_iota(jnp.int32, shape, dim)


def _tril_ones(n):
    return (_iota2((n, n), 0) >= _iota2((n, n), 1)).astype(F32)


def _inproj_kernel(x_ref, w_ref, ws_ref, y_ref, ys_ref, xb_ref):
    j = pl.program_id(1)

    @pl.when(j == 0)
    def _():
        xb = x_ref[...].astype(BF16)
        xb_ref[...] = xb
        ys_ref[...] = _dot(xb, ws_ref[...])

    y_ref[...] = _dot(xb_ref[...], w_ref[...])


def _inproj(x2, w, ws, tm=1024, tn=512):
    n, k = x2.shape
    m = w.shape[1]
    return pl.pallas_call(
        _inproj_kernel,
        grid=(n // tm, m // tn),
        in_specs=[pl.BlockSpec((tm, k), lambda i, j: (i, 0)),
                  pl.BlockSpec((k, tn), lambda i, j: (0, j)),
                  pl.BlockSpec((k, LANE), lambda i, j: (0, 0))],
        out_specs=[pl.BlockSpec((tm, tn), lambda i, j: (i, j)),
                   pl.BlockSpec((tm, LANE), lambda i, j: (i, 0))],
        out_shape=[jax.ShapeDtypeStruct((n, m), F32), jax.ShapeDtypeStruct((n, LANE), F32)],
        scratch_shapes=[pltpu.VMEM((tm, k), BF16)],
        compiler_params=_cparams(("parallel", "arbitrary")),
        name="inproj",
    )(x2, w, ws)


def _outproj_ln_kernel(o_ref, w_ref, x_ref, g_ref, b_ref, y_ref):
    m = _dot(o_ref[...], w_ref[...])
    y_ref[...] = _layer_norm(DN_ALPHA * x_ref[...] + m, g_ref[...], b_ref[...])


def _outproj_ln(o2, w, x2, g, b, tm=512):
    n, k = o2.shape
    d = w.shape[1]
    return pl.pallas_call(
        _outproj_ln_kernel,
        grid=(n // tm,),
        in_specs=[pl.BlockSpec((tm, k), lambda i: (i, 0)),
                  pl.BlockSpec((k, d), lambda i: (0, 0)),
                  pl.BlockSpec((tm, d), lambda i: (i, 0)),
                  pl.BlockSpec((1, d), lambda i: (0, 0)),
                  pl.BlockSpec((1, d), lambda i: (0, 0))],
        out_specs=pl.BlockSpec((tm, d), lambda i: (i, 0)),
        out_shape=jax.ShapeDtypeStruct((n, d), F32),
        compiler_params=_cparams(("parallel",)),
        name="outproj_ln",
    )(o2, w, x2, g, b)


def _ffn_kernel(x_ref, wg_ref, wu_ref, wd_ref, g_ref, b_ref, y_ref, xb_ref, acc_ref):
    f = pl.program_id(1)

    @pl.when(f == 0)
    def _():
        xb_ref[...] = x_ref[...].astype(BF16)
        acc_ref[...] = jnp.zeros_like(acc_ref)

    xb = xb_ref[...]
    gate = _dot(xb, wg_ref[...])
    up = _dot(xb, wu_ref[...])
    h = (_silu(gate) * up).astype(BF16)
    acc_ref[...] += _dot(h, wd_ref[...])

    @pl.when(f == pl.num_programs(1) - 1)
    def _():
        y_ref[...] = _layer_norm(DN_ALPHA * x_ref[...] + acc_ref[...], g_ref[...], b_ref[...])


def _ffn(x2, w_in, w_down, g, b, tm=1024, tf=256):
    n, d = x2.shape
    nf = D_FF // tf
    return pl.pallas_call(
        _ffn_kernel,
        grid=(n // tm, nf),
        in_specs=[pl.BlockSpec((tm, d), lambda i, f: (i, 0)),
                  pl.BlockSpec((d, tf), lambda i, f: (0, f)),
                  pl.BlockSpec((d, tf), lambda i, f: (0, f + nf)),
                  pl.BlockSpec((tf, d), lambda i, f: (f, 0)),
                  pl.BlockSpec((1, d), lambda i, f: (0, 0)),
                  pl.BlockSpec((1, d), lambda i, f: (0, 0))],
        out_specs=pl.BlockSpec((tm, d), lambda i, f: (i, 0)),
        out_shape=jax.ShapeDtypeStruct((n, d), F32),
        scratch_shapes=[pltpu.VMEM((tm, d), BF16), pltpu.VMEM((tm, d), F32)],
        compiler_params=_cparams(("parallel", "arbitrary")),
        name="ffn",
    )(x2, w_in, w_in, w_down, g, b)


def _gated_rms_norm(o, gate, w):
    y = o * lax.rsqrt(jnp.mean(o * o, axis=-1, keepdims=True) + RMS_EPS) * w
    return y * _silu(gate)


def _pair_block_diag(x, lane_split):
    lane = _iota2(x.shape, 1)
    zero = jnp.zeros_like(x)
    return jnp.concatenate([jnp.where(lane < lane_split, x, zero), jnp.where(lane >= lane_split, x, zero)], axis=0)


def _pair_unit_lower_inverse(l_ps, cj_sel, nchunk):
    n = nchunk * CHUNK
    w = 2 * CHUNK
    row = _iota2((n, w), 0) & (CHUNK - 1)
    col = _iota2((n, w), 1) & (CHUNK - 1)
    eye = (row == col).astype(F32).reshape(n // GSUB, GSUB, w)
    diag_blk = (row >> GSUB_SHIFT) == (col >> GSUB_SHIFT)
    cj_alls = [_dot(jnp.where(diag_blk, l_p, 0.0).astype(BF16), cj_sel) for l_p in l_ps]
    xs, ls = [], []
    for l_p, cj_all in zip(l_ps, cj_alls):
        x = eye
        for j in range(GSUB - 1):
            cj = cj_all[:, j * w:(j + 1) * w].reshape(n // GSUB, GSUB, w)
            x = x - cj * x[:, j:j + 1, :]
        x = x.reshape(n, w)
        for c in range(nchunk):
            xs.append(x[c * CHUNK:(c + 1) * CHUNK])
            ls.append(l_p[c * CHUNK:(c + 1) * CHUNK])
    r64 = _iota2((CHUNK, w), 0)
    c64 = _iota2((CHUNK, w), 1) & (CHUNK - 1)
    shift = GSUB_SHIFT
    while (1 << shift) < CHUNK:
        sel = ((r64 >> (shift + 1)) == (c64 >> (shift + 1))) & ((r64 >> shift) != (c64 >> shift))
        x16 = [x.astype(BF16) for x in xs]
        m1 = [_dot(jnp.where(sel, lc, 0.0).astype(BF16), _pair_block_diag(xb, CHUNK)).astype(BF16)
              for lc, xb in zip(ls, x16)]
        xs = [x - _dot(xb, _pair_block_diag(m, CHUNK)) for x, xb, m in zip(xs, x16, m1)]
        shift += 1
    return xs


def _gdn_kernel(q_ref, k_ref, v_ref, z_ref, ba_ref, cwq_ref, cwk_ref, cwv_ref, par_ref, nw_ref, sel_ref,
                o_ref,
                s_ref, wq_ref, wk_ref, wv_ref, pq_ref, n_ref, oc_ref, gl_ref, *, hb, tc):
    hblk = pl.program_id(1)
    t = pl.program_id(2)
    dk = LANE
    nchunk = tc // CHUNK

    @pl.when(t == 0)
    def _():
        s_ref[...] = jnp.zeros_like(s_ref)
        wq_ref[0:8, :] = jnp.zeros((8, hb * dk), F32)
        wk_ref[0:8, :] = jnp.zeros((8, hb * dk), F32)
        wv_ref[0:8, :] = jnp.zeros((8, hb * dk), F32)

    def conv_silu(src_ref, win_ref, cw_ref):
        win_ref[8:16, :] = src_ref[0, 0:8, :]
        head = cw_ref[3:4, :] * win_ref[8:16, :]
        body = cw_ref[3:4, :] * src_ref[0, 8:tc, :]
        for j in range(3):
            head = head + cw_ref[j:j + 1, :] * win_ref[5 + j:13 + j, :]
            body = body + cw_ref[j:j + 1, :] * src_ref[0, 5 + j:tc - 3 + j, :]
        win_ref[0:8, :] = src_ref[0, tc - 8:tc, :]
        return _silu(jnp.concatenate([head, body], axis=0))

    qc = conv_silu(q_ref, wq_ref, cwq_ref)
    kc = conv_silu(k_ref, wk_ref, cwk_ref)
    vc = conv_silu(v_ref, wv_ref, cwv_ref)

    ba = ba_ref[0]
    beta_all = _sigmoid(ba)
    g_all = -jnp.exp(par_ref[0:1, :]) * _softplus(ba + par_ref[1:2, :])
    big_all = _chunk_cumsum(g_all, tc)
    big_all_t = big_all.T

    lane = _iota2((tc, LANE), 1)
    subl = _iota2((LANE, tc), 0)
    first = lane < CHUNK
    rloc = _iota2((tc, LANE), 0) & (CHUNK - 1)
    cloc = lane & (CHUNK - 1)

    gcs, betas, grows, qs, ks, vs = [], [], [], [], [], []
    for hh in range(hb):
        sl = slice(hh * dk, (hh + 1) * dk)
        h = hblk * hb + hh
        gcs.append(jnp.sum(jnp.where(lane == GDN_HEADS + h, big_all, 0.0), axis=-1, keepdims=True))
        betas.append(jnp.sum(jnp.where(lane == h, beta_all, 0.0), axis=-1, keepdims=True))
        grows.append(jnp.sum(jnp.where(subl == GDN_HEADS + h, big_all_t, 0.0), axis=0, keepdims=True))
        qh = qc[:, sl]
        kh = kc[:, sl]
        qs.append(qh * (lax.rsqrt(jnp.sum(qh * qh, axis=-1, keepdims=True) + L2_EPS) * (dk ** -0.5)))
        ks.append(kh * lax.rsqrt(jnp.sum(kh * kh, axis=-1, keepdims=True) + L2_EPS))
        vs.append(vc[:, sl])

    lane1 = _iota2((1, LANE), 1)
    l_ps, a16s = [], []
    for pr in range(hb // 2):
        ha, hbb = 2 * pr, 2 * pr + 1
        gc_p = jnp.where(first, gcs[ha], gcs[hbb])
        beta_p = jnp.where(first, betas[ha], betas[hbb])
        gr_rows = []
        for c in range(nchunk):
            vsl = slice((c // 2) * LANE, (c // 2 + 1) * LANE)
            ra, rb = grows[ha][:, vsl], grows[hbb][:, vsl]
            if c % 2 == 0:
                rb = pltpu.roll(rb, CHUNK, 1)
            else:
                ra = pltpu.roll(ra, CHUNK, 1)
            gr_rows.append(jnp.broadcast_to(jnp.where(lane1 < CHUNK, ra, rb), (CHUNK, LANE)))
        gr_p = jnp.concatenate(gr_rows, axis=0)
        decay_p = jnp.exp(jnp.where(rloc >= cloc, gc_p - gr_p, -jnp.inf))
        q16 = jnp.concatenate([qs[ha], qs[hbb]], axis=1).astype(BF16)
        k16 = jnp.concatenate([ks[ha], ks[hbb]], axis=1).astype(BF16)
        rr = [_dot_nt(jnp.concatenate([q16[c * CHUNK:(c + 1) * CHUNK], k16[c * CHUNK:(c + 1) * CHUNK]], axis=0),
                      _pair_block_diag(k16[c * CHUNK:(c + 1) * CHUNK], dk)) for c in range(nchunk)]
        qk_p = jnp.concatenate([r[:CHUNK] for r in rr], axis=0)
        kk_p = jnp.concatenate([r[CHUNK:] for r in rr], axis=0)
        l_ps.append(jnp.where(rloc > cloc, kk_p * beta_p * decay_p, 0.0))
        a16s.append((qk_p * decay_p).astype(BF16))
    tinv = _pair_unit_lower_inverse(l_ps, sel_ref[...], nchunk)

    rhs16, kds, qds, egl = [], [], [], []
    for hh in range(hb):
        eg = jnp.exp(gcs[hh])
        rhs16.append(jnp.concatenate([vs[hh] * betas[hh], ks[hh] * (betas[hh] * eg)], axis=1).astype(BF16))
        g_last = jnp.concatenate(
            [jnp.broadcast_to(gcs[hh][(c + 1) * CHUNK - 1:(c + 1) * CHUNK, :], (CHUNK, 1)) for c in range(nchunk)],
            axis=0)
        kds.append(ks[hh] * jnp.exp(g_last - gcs[hh]))
        qds.append(qs[hh] * eg)
        egl.append(jnp.exp(g_last))
    zeros_rhs = jnp.zeros((CHUNK, 2 * dk), BF16)

    def pair_bd(left, right):
        return jnp.concatenate([jnp.concatenate([left, zeros_rhs], axis=1),
                                jnp.concatenate([zeros_rhs, right], axis=1)], axis=0)

    pcs = [(pr, c) for pr in range(hb // 2) for c in range(nchunk)]
    rows = lambda c: slice(c * CHUNK, (c + 1) * CHUNK)
    uw16 = [_dot(tinv[i].astype(BF16), pair_bd(rhs16[2 * pr][rows(c)], rhs16[2 * pr + 1][rows(c)])).astype(BF16)
            for i, (pr, c) in enumerate(pcs)]
    a_uw = [_dot(a16s[pr][rows(c)], pair_bd(uw16[i][:, :2 * dk], uw16[i][:, 2 * dk:]))
            for i, (pr, c) in enumerate(pcs)]
    for i, (pr, c) in enumerate(pcs):
        for side in range(2):
            hh = 2 * pr + side
            off = side * 2 * dk
            k_uw = _dot(kds[hh][rows(c)].T.astype(BF16), uw16[i][:, off:off + 2 * dk])
            pq_ref[hh, c, 0:dk, :] = k_uw[:, dk:].astype(BF16)
            pq_ref[hh, c, dk:dk + CHUNK, :] = (qds[hh][rows(c)] - a_uw[i][:, off + dk:off + 2 * dk]).astype(BF16)
            n_ref[hh, c] = k_uw[:, :dk]
            oc_ref[hh, c] = a_uw[i][:, off:off + dk]
            gl_ref[hh, c] = jnp.broadcast_to(egl[hh][(c + 1) * CHUNK - 1:(c + 1) * CHUNK, :], (8, LANE))

    def scan(c, carry):
        for hh in range(hb):
            s = s_ref[hh]
            res = _dot(pq_ref[hh, c], s.astype(BF16))
            s_ref[hh] = s * gl_ref[hh, c, 0:1, :] - res[:dk] + n_ref[hh, c]
            oc_ref[hh, c] = res[dk:] + oc_ref[hh, c]
        return carry

    lax.fori_loop(0, nchunk, scan, 0)

    for hh in range(hb):
        sl = slice(hh * dk, (hh + 1) * dk)
        o = oc_ref[hh].reshape(tc, dk)
        o_ref[0, :, sl] = _gated_rms_norm(o, z_ref[0, :, sl], nw_ref[...]).astype(o_ref.dtype)


def _gdn_cj_selector():
    w = 2 * CHUNK
    i = jnp.arange(w)[:, None]
    c = jnp.arange(w)[None, :]
    blocks = [((i // CHUNK) == (c // CHUNK)) & ((i % GSUB) == j) for j in range(GSUB - 1)]
    return jnp.concatenate(blocks, axis=1).astype(BF16)


def _gdn_mixer(p, ba, conv_w, par, norm_w, batch, seq, hb=4, tc=256):
    dk = LANE
    assert hb % 2 == 0 and GDN_HEADS % hb == 0
    w = hb * dk
    nhb = GDN_HEADS // hb
    nchunk = tc // CHUNK
    col = lambda off: (lambda b, h, t: (b, t, off * nhb + h))
    cwcol = lambda off: (lambda b, h, t: (0, off * nhb + h))
    sel = _gdn_cj_selector()
    kern = functools.partial(_gdn_kernel, hb=hb, tc=tc)
    return pl.pallas_call(
        kern,
        grid=(batch, nhb, seq // tc),
        in_specs=[pl.BlockSpec((1, tc, w), col(0)),
                  pl.BlockSpec((1, tc, w), col(1)),
                  pl.BlockSpec((1, tc, w), col(2)),
                  pl.BlockSpec((1, tc, w), col(3)),
                  pl.BlockSpec((1, tc, LANE), lambda b, h, t: (b, t, 0)),
                  pl.BlockSpec((4, w), cwcol(0)),
                  pl.BlockSpec((4, w), cwcol(1)),
                  pl.BlockSpec((4, w), cwcol(2)),
                  pl.BlockSpec((2, LANE), lambda b, h, t: (0, 0)),
                  pl.BlockSpec((1, dk), lambda b, h, t: (0, 0)),
                  pl.BlockSpec(sel.shape, lambda b, h, t: (0, 0))],
        out_specs=pl.BlockSpec((1, tc, w), lambda b, h, t: (b, t, h)),
        out_shape=jax.ShapeDtypeStruct((batch, seq, GDN_HEADS * dk), BF16),
        scratch_shapes=[pltpu.VMEM((hb, dk, dk), F32),
                        pltpu.VMEM((16, w), F32),
                        pltpu.VMEM((16, w), F32),
                        pltpu.VMEM((16, w), F32),
                        pltpu.VMEM((hb, nchunk, dk + CHUNK, dk), BF16),
                        pltpu.VMEM((hb, nchunk, dk, dk), F32),
                        pltpu.VMEM((hb, nchunk, CHUNK, dk), F32),
                        pltpu.VMEM((hb, nchunk, 8, LANE), F32)],
        compiler_params=_cparams(("parallel", "parallel", "arbitrary")),
        name="gdn_recurrence",
    )(p, p, p, p, ba, conv_w, conv_w, conv_w, par, norm_w, sel)


def _gla_pair_scores(q, k, big, nchunk):
    n, wk = q.shape
    dk = wk // 2
    row = _iota2((n, wk), 0)
    rl = _iota2((n, 2 * CHUNK), 0) & (CHUNK - 1)
    lane = _iota2((n, 2 * CHUNK), 1)
    cl = lane & (CHUNK - 1)
    a_p = jnp.zeros((n, 2 * CHUNK), F32)

    w = CHUNK // 2
    shift = CHUNK_SHIFT - 1
    while w >= SUB:
        gb = jnp.broadcast_to(big.reshape(n // (2 * w), 2 * w, wk)[:, w - 1:w, :], (n // (2 * w), 2 * w, wk))
        gb = gb.reshape(n, wk)
        upper = (row & (2 * w - 1)) >= w
        e = jnp.exp(jnp.where(upper, big - gb, gb - big))
        qe = (q * e).astype(BF16)
        ke = (k * e).astype(BF16)
        lvl = jnp.concatenate(
            [_dot_nt(qe[c * CHUNK:(c + 1) * CHUNK], _pair_block_diag(ke[c * CHUNK:(c + 1) * CHUNK], dk))
             for c in range(nchunk)], axis=0)
        valid = ((rl & (2 * w - 1)) >= w) & ((cl & (2 * w - 1)) < w) & ((rl >> (shift + 1)) == (cl >> (shift + 1)))
        a_p = jnp.where(valid, lvl, a_p)
        w //= 2
        shift -= 1

    rsub = row & (SUB - 1)
    for d in range(SUB):
        if d == 0:
            p = q * k
        else:
            kd = pltpu.roll(k, d, 0)
            gd = pltpu.roll(big, d, 0)
            p = q * kd * jnp.exp(jnp.where(rsub >= d, big - gd, -jnp.inf))
        diag_a = jnp.sum(p[:, :dk], axis=-1, keepdims=True)
        diag_b = jnp.sum(p[:, dk:], axis=-1, keepdims=True)
        on_diag = (cl == rl - d) & ((rl & (SUB - 1)) >= d)
        a_p = jnp.where(on_diag, jnp.where(lane < CHUNK, diag_a, diag_b), a_p)
    return a_p


def _gla_kernel(q_ref, k_ref, v_ref, r_ref, aux_ref, w2_ref, b2_ref, nw_ref,
                o_ref,
                s_ref, *, tc, dv, mode, layer):
    t = pl.program_id(2)
    dk = LANE
    hb = 2
    nchunk = tc // CHUNK

    @pl.when(t == 0)
    def _():
        s_ref[...] = jnp.zeros_like(s_ref)

    if mode == "hgrn":
        lg = aux_ref[...]
        e = jnp.exp(lg - jnp.max(lg, axis=0, keepdims=True))
        soft = e / jnp.sum(e, axis=0, keepdims=True)
        lb = jnp.zeros((1, hb * dk), F32)
        for j in range(1, layer + 1):
            lb = lb + soft[j:j + 1, :]
        f_raw = k_ref[0]
        et = jnp.exp(-jnp.abs(f_raw))
        bterm = jnp.log1p(-lb) + (jnp.minimum(f_raw, 0.0) - _log1p_unit(et))
        log_lb = jnp.log(lb)
        g = jnp.maximum(log_lb, bterm) + _log1p_unit(jnp.exp(-jnp.abs(log_lb - bterm)))
        k = (1.0 - lb) * (jnp.where(f_raw >= 0.0, et, 1.0) / (1.0 + et))
        q = _silu(q_ref[0]) * (dk ** -0.5)
    else:
        x = _dot(aux_ref[0], w2_ref[...], HI) + b2_ref[...]
        g = _log_sigmoid(x) / GLA_TAU
        k = k_ref[0]
        q = q_ref[0] * (dk ** -0.5)

    big = _chunk_cumsum(g, tc)
    g_last = jnp.broadcast_to(big.reshape(nchunk, CHUNK, hb * dk)[:, CHUNK - 1:CHUNK, :],
                              (nchunk, CHUNK, hb * dk)).reshape(tc, hb * dk)
    q_dec = (q * jnp.exp(big)).astype(BF16)
    k_dec = (k * jnp.exp(g_last - big)).astype(BF16)
    chunk_decay = jnp.exp(g_last)
    a16 = _gla_pair_scores(q, k, big, nchunk).astype(BF16)

    v16 = v_ref[0].astype(BF16)
    zeros_v = jnp.zeros((CHUNK, dv), BF16)
    states = [s_ref[hh] for hh in range(hb)]
    outs = [[] for _ in range(hb)]
    for c in range(nchunk):
        rs = slice(c * CHUNK, (c + 1) * CHUNK)
        v_bd = jnp.concatenate([jnp.concatenate([v16[rs, :dv], zeros_v], axis=1),
                                jnp.concatenate([zeros_v, v16[rs, dv:]], axis=1)], axis=0)
        intra = _dot(a16[rs], v_bd)
        for hh in range(hb):
            sl = slice(hh * dk, (hh + 1) * dk)
            slv = slice(hh * dv, (hh + 1) * dv)
            s = states[hh]
            outs[hh].append(_dot_nt(q_dec[rs, sl], s.astype(BF16)) + intra[:, slv])
            kv = lax.dot_general(v16[rs, slv], k_dec[rs, sl], (((0,), (0,)), ((), ())), preferred_element_type=F32)
            states[hh] = s * chunk_decay[(c + 1) * CHUNK - 1:(c + 1) * CHUNK, sl] + kv
    for hh in range(hb):
        slv = slice(hh * dv, (hh + 1) * dv)
        s_ref[hh] = states[hh]
        o = jnp.concatenate(outs[hh], axis=0)
        o_ref[0, :, slv] = _gated_rms_norm(o, r_ref[0, :, slv], nw_ref[...]).astype(o_ref.dtype)


def _gla_mixer(p, aux, w2, b2, norm_w, batch, seq, *, mode, layer, heads, dv, cols, tc=256):
    dk = LANE
    hb = 2
    wk = hb * dk
    wv = hb * dv
    nhb = heads // hb
    qo, ko, vo, ro = cols
    kern = functools.partial(_gla_kernel, tc=tc, dv=dv, mode=mode, layer=layer)
    if mode == "hgrn":
        aux_spec = pl.BlockSpec((DEPTH, wk), lambda b, h, t: (0, h))
    else:
        aux_spec = pl.BlockSpec((1, tc, LANE), lambda b, h, t: (b, t, 0))
    return pl.pallas_call(
        kern,
        grid=(batch, nhb, seq // tc),
        in_specs=[pl.BlockSpec((1, tc, wk), lambda b, h, t: (b, t, qo * nhb + h)),
                  pl.BlockSpec((1, tc, wk), lambda b, h, t: (b, t, ko * nhb + h)),
                  pl.BlockSpec((1, tc, wv), lambda b, h, t: (b, t, vo * nhb + h)),
                  pl.BlockSpec((1, tc, wv), lambda b, h, t: (b, t, ro * nhb + h)),
                  aux_spec,
                  pl.BlockSpec((LANE, wk), lambda b, h, t: (0, h)),
                  pl.BlockSpec((1, wk), lambda b, h, t: (0, h)),
                  pl.BlockSpec((1, dv), lambda b, h, t: (0, 0))],
        out_specs=pl.BlockSpec((1, tc, wv), lambda b, h, t: (b, t, h)),
        out_shape=jax.ShapeDtypeStruct((batch, seq, heads * dv), BF16),
        scratch_shapes=[pltpu.VMEM((hb, dv, dk), F32)],
        compiler_params=_cparams(("parallel", "parallel", "arbitrary")),
        name=mode + "_recurrence",
    )(p, p, p, p, aux, w2, b2, norm_w)


def _pad_cols(w, width=LANE):
    return jnp.pad(w, ((0, 0), (0, width - w.shape[1])))


def kernel(x, gdn_w_in, gdn_conv_w, gdn_a_log, gdn_dt_bias, gdn_norm_w, gdn_w_out, hgrn_w_in, hgrn_lb_logits, hgrn_norm_w, hgrn_w_out, gla_w_in, gla_w_gk2, gla_b_gk, gla_norm_w, gla_w_out, mix_ln_g, mix_ln_b, ffn_w_in, ffn_w_down, ffn_ln_g, ffn_ln_b):
    batch, seq, d = x.shape
    n = batch * seq
    x2 = x.reshape(n, d)
    zero_side = jnp.zeros((d, LANE), BF16)
    for i in range(DEPTH):
        kind, j = i % 3, i // 3
        if kind == 0:
            w_in = gdn_w_in[j]
            p, ba = _inproj(x2, w_in[:, :4096].astype(BF16), _pad_cols(w_in[:, 4096:]).astype(BF16))
            par = jnp.zeros((2, LANE), F32)
            par = par.at[0, GDN_HEADS:2 * GDN_HEADS].set(gdn_a_log[j]).at[1, GDN_HEADS:2 * GDN_HEADS].set(gdn_dt_bias[j])
            o = _gdn_mixer(p.reshape(batch, seq, 4096), ba.reshape(batch, seq, LANE), gdn_conv_w[j], par,
                           gdn_norm_w[j].reshape(1, -1), batch, seq)
            w_out = gdn_w_out[j]
        elif kind == 1:
            p, _ = _inproj(x2, hgrn_w_in[j].astype(BF16), zero_side)
            o = _gla_mixer(p.reshape(batch, seq, 4096), hgrn_lb_logits, jnp.zeros((LANE, D_MODEL), F32),
                           jnp.zeros((1, D_MODEL), F32), hgrn_norm_w[j].reshape(1, -1), batch, seq,
                           mode="hgrn", layer=i, heads=8, dv=128, cols=(0, 1, 2, 3))
            w_out = hgrn_w_out[j]
        else:
            w_in = gla_w_in[j]
            p, gk = _inproj(x2, w_in[:, :3072].astype(BF16), _pad_cols(w_in[:, 3072:]).astype(BF16))
            w2 = jnp.pad(gla_w_gk2[j], ((0, LANE - gla_w_gk2.shape[1]), (0, 0)))
            o = _gla_mixer(p.reshape(batch, seq, 3072), gk.reshape(batch, seq, LANE), w2,
                           gla_b_gk[j].reshape(1, -1), gla_norm_w[j].reshape(1, -1), batch, seq,
                           mode="gla", layer=i, heads=4, dv=256, cols=(0, 1, 1, 2))
            w_out = gla_w_out[j]
        x2 = _outproj_ln(o.reshape(n, -1), w_out.astype(BF16), x2, mix_ln_g[i].reshape(1, -1), mix_ln_b[i].reshape(1, -1))
        x2 = _ffn(x2, ffn_w_in[i].astype(BF16), ffn_w_down[i].astype(BF16), ffn_ln_g[i].reshape(1, -1), ffn_ln_b[i].reshape(1, -1))
    return x2.reshape(batch, seq, d)
```

```python
import functools
import math

import jax
import jax.numpy as jnp
from jax import lax
from jax.experimental import pallas as pl
from jax.experimental.pallas import tpu as pltpu

F32 = jnp.float32
BF16 = jnp.bfloat16
HI = lax.Precision.HIGHEST

D_MODEL = 1024
DEPTH = 4
CHUNK = 64
CHUNK_SHIFT = 6
SUB = 4
GSUB = 8
GSUB_SHIFT = 3
LN_EPS = 1e-5
RMS_EPS = 1e-6
L2_EPS = 1e-6
DN_ALPHA = (2 * DEPTH) ** 0.25
D_FF = 2816
LANE = 128
GDN_HEADS = 8
GLA_TAU = 16.0

VMEM_LIMIT = 56 * 1024 * 1024


def _cparams(sem):
    return pltpu.CompilerParams(dimension_semantics=sem, vmem_limit_bytes=VMEM_LIMIT)


def _dot(a, b, precision=None):
    return jnp.dot(a, b, preferred_element_type=F32, precision=precision)


def _dot_nt(a, b, precision=None):
    return lax.dot_general(a, b, (((1,), (1,)), ((), ())), preferred_element_type=F32, precision=precision)


def _sigmoid(x):
    return 1.0 / (1.0 + jnp.exp(-x))


def _silu(x):
    return x * _sigmoid(x)


def _log1p_unit(x):
    return jnp.log(1.0 + x)


def _log_sigmoid(x):
    return jnp.minimum(x, 0.0) - _log1p_unit(jnp.exp(-jnp.abs(x)))


def _softplus(x):
    return jnp.maximum(x, 0.0) + _log1p_unit(jnp.exp(-jnp.abs(x)))


def _chunk_cumsum(g, tc):
    rt = _iota2((tc, tc), 0)
    ct = _iota2((tc, tc), 1)
    tri = ((rt >= ct) & ((rt >> CHUNK_SHIFT) == (ct >> CHUNK_SHIFT))).astype(BF16)
    hi = g.astype(BF16)
    r1 = g - hi.astype(F32)
    mid = r1.astype(BF16)
    lo = (r1 - mid.astype(F32)).astype(BF16)
    return _dot(tri, hi) + _dot(tri, mid) + _dot(tri, lo)


def _layer_norm(h, g, b):
    mu = jnp.mean(h, axis=-1, keepdims=True)
    d = h - mu
    var = jnp.mean(d * d, axis=-1, keepdims=True)
    return d * lax.rsqrt(var + LN_EPS) * g + b


def _iota2(shape, dim):
    return lax.broadcasted_iota(jnp.int32, shape, dim)


def _tril_ones(n):
    return (_iota2((n, n), 0) >= _iota2((n, n), 1)).astype(F32)


def _outproj_ln_kernel(o_ref, w_ref, x_ref, g_ref, b_ref, y_ref):
    m = _dot(o_ref[...], w_ref[...])
    y_ref[...] = _layer_norm(DN_ALPHA * x_ref[...] + m, g_ref[...], b_ref[...])


def _outproj_ln(o2, w, x2, g, b, tm=512):
    n, k = o2.shape
    d = w.shape[1]
    return pl.pallas_call(
        _outproj_ln_kernel,
        grid=(n // tm,),
        in_specs=[pl.BlockSpec((tm, k), lambda i: (i, 0)),
                  pl.BlockSpec((k, d), lambda i: (0, 0)),
                  pl.BlockSpec((tm, d), lambda i: (i, 0)),
                  pl.BlockSpec((1, d), lambda i: (0, 0)),
                  pl.BlockSpec((1, d), lambda i: (0, 0))],
        out_specs=pl.BlockSpec((tm, d), lambda i: (i, 0)),
        out_shape=jax.ShapeDtypeStruct((n, d), F32),
        compiler_params=_cparams(("parallel",)),
        name="outproj_ln",
    )(o2, w, x2, g, b)


def _ffn_kernel(x_ref, wg_ref, wu_ref, wd_ref, g_ref, b_ref, y_ref, xb_ref, acc_ref):
    f = pl.program_id(1)

    @pl.when(f == 0)
    def _():
        xb_ref[...] = x_ref[...].astype(BF16)
        acc_ref[...] = jnp.zeros_like(acc_ref)

    xb = xb_ref[...]
    gate = _dot(xb, wg_ref[...])
    up = _dot(xb, wu_ref[...])
    h = (_silu(gate) * up).astype(BF16)
    acc_ref[...] += _dot(h, wd_ref[...])

    @pl.when(f == pl.num_programs(1) - 1)
    def _():
        y_ref[...] = _layer_norm(DN_ALPHA * x_ref[...] + acc_ref[...], g_ref[...], b_ref[...])


def _ffn(x2, w_in, w_down, g, b, tm=1024, tf=256):
    n, d = x2.shape
    nf = D_FF // tf
    return pl.pallas_call(
        _ffn_kernel,
        grid=(n // tm, nf),
        in_specs=[pl.BlockSpec((tm, d), lambda i, f: (i, 0)),
                  pl.BlockSpec((d, tf), lambda i, f: (0, f)),
                  pl.BlockSpec((d, tf), lambda i, f: (0, f + nf)),
                  pl.BlockSpec((tf, d), lambda i, f: (f, 0)),
                  pl.BlockSpec((1, d), lambda i, f: (0, 0)),
                  pl.BlockSpec((1, d), lambda i, f: (0, 0))],
        out_specs=pl.BlockSpec((tm, d), lambda i, f: (i, 0)),
        out_shape=jax.ShapeDtypeStruct((n, d), F32),
        scratch_shapes=[pltpu.VMEM((tm, d), BF16), pltpu.VMEM((tm, d), F32)],
        compiler_params=_cparams(("parallel", "arbitrary")),
        name="ffn",
    )(x2, w_in, w_in, w_down, g, b)


def _gated_rms_norm(o, gate, w):
    y = o * lax.rsqrt(jnp.mean(o * o, axis=-1, keepdims=True) + RMS_EPS) * w
    return y * _silu(gate)


def _pair_block_diag(x, lane_split):
    lane = _iota2(x.shape, 1)
    zero = jnp.zeros_like(x)
    return jnp.concatenate([jnp.where(lane < lane_split, x, zero), jnp.where(lane >= lane_split, x, zero)], axis=0)


def _pair_unit_lower_inverse(l_ps, cj_sel, nchunk):
    n = nchunk * CHUNK
    w = 2 * CHUNK
    row = _iota2((n, w), 0) & (CHUNK - 1)
    col = _iota2((n, w), 1) & (CHUNK - 1)
    eye = (row == col).astype(F32).reshape(n // GSUB, GSUB, w)
    diag_blk = (row >> GSUB_SHIFT) == (col >> GSUB_SHIFT)
    cj_alls = [_dot(jnp.where(diag_blk, l_p, 0.0).astype(BF16), cj_sel) for l_p in l_ps]
    xs, ls = [], []
    for l_p, cj_all in zip(l_ps, cj_alls):
        x = eye
        for j in range(GSUB - 1):
            cj = cj_all[:, j * w:(j + 1) * w].reshape(n // GSUB, GSUB, w)
            x = x - cj * x[:, j:j + 1, :]
        x = x.reshape(n, w)
        for c in range(nchunk):
            xs.append(x[c * CHUNK:(c + 1) * CHUNK])
            ls.append(l_p[c * CHUNK:(c + 1) * CHUNK])
    r64 = _iota2((CHUNK, w), 0)
    c64 = _iota2((CHUNK, w), 1) & (CHUNK - 1)
    shift = GSUB_SHIFT
    while (1 << shift) < CHUNK:
        sel = ((r64 >> (shift + 1)) == (c64 >> (shift + 1))) & ((r64 >> shift) != (c64 >> shift))
        x16 = [x.astype(BF16) for x in xs]
        m1 = [_dot(jnp.where(sel, lc, 0.0).astype(BF16), _pair_block_diag(xb, CHUNK)).astype(BF16)
              for lc, xb in zip(ls, x16)]
        xs = [x - _dot(xb, _pair_block_diag(m, CHUNK)) for x, xb, m in zip(xs, x16, m1)]
        shift += 1
    return xs


def _gdn_kernel(x_ref, w_ref, ws_ref, cwq_ref, cwk_ref, cwv_ref, par_ref, nw_ref, sel_ref,
                o_ref,
                s_ref, xb_ref, rq_ref, rk_ref, rv_ref, wq_ref, wk_ref, wv_ref, pq_ref, n_ref, oc_ref, gl_ref,
                *, hb, tc):
    hblk = pl.program_id(1)
    t = pl.program_id(2)
    dk = LANE
    nchunk = tc // CHUNK

    @pl.when(t == 0)
    def _():
        s_ref[...] = jnp.zeros_like(s_ref)
        wq_ref[0:8, :] = jnp.zeros((8, hb * dk), F32)
        wk_ref[0:8, :] = jnp.zeros((8, hb * dk), F32)
        wv_ref[0:8, :] = jnp.zeros((8, hb * dk), F32)

    w = hb * dk
    xb_ref[...] = x_ref[0].astype(BF16)
    rq_ref[...] = _dot(xb_ref[...], w_ref[0, :, 0:w])
    rk_ref[...] = _dot(xb_ref[...], w_ref[0, :, w:2 * w])
    rv_ref[...] = _dot(xb_ref[...], w_ref[0, :, 2 * w:3 * w])
    ba = _dot(xb_ref[...], ws_ref[...])

    def conv_silu(src_ref, win_ref, cw_ref):
        win_ref[8:16, :] = src_ref[0:8, :]
        head = cw_ref[3:4, :] * win_ref[8:16, :]
        body = cw_ref[3:4, :] * src_ref[8:tc, :]
        for j in range(3):
            head = head + cw_ref[j:j + 1, :] * win_ref[5 + j:13 + j, :]
            body = body + cw_ref[j:j + 1, :] * src_ref[5 + j:tc - 3 + j, :]
        win_ref[0:8, :] = src_ref[tc - 8:tc, :]
        return _silu(jnp.concatenate([head, body], axis=0))

    qc = conv_silu(rq_ref, wq_ref, cwq_ref)
    kc = conv_silu(rk_ref, wk_ref, cwk_ref)
    vc = conv_silu(rv_ref, wv_ref, cwv_ref)

    beta_all = _sigmoid(ba)
    g_all = -jnp.exp(par_ref[0:1, :]) * _softplus(ba + par_ref[1:2, :])
    big_all = _chunk_cumsum(g_all, tc)
    big_all_t = big_all.T

    lane = _iota2((tc, LANE), 1)
    subl = _iota2((LANE, tc), 0)
    first = lane < CHUNK
    rloc = _iota2((tc, LANE), 0) & (CHUNK - 1)
    cloc = lane & (CHUNK - 1)

    gcs, betas, grows, qs, ks, vs = [], [], [], [], [], []
    for hh in range(hb):
        sl = slice(hh * dk, (hh + 1) * dk)
        h = hblk * hb + hh
        gcs.append(jnp.sum(jnp.where(lane == GDN_HEADS + h, big_all, 0.0), axis=-1, keepdims=True))
        betas.append(jnp.sum(jnp.where(lane == h, beta_all, 0.0), axis=-1, keepdims=True))
        grows.append(jnp.sum(jnp.where(subl == GDN_HEADS + h, big_all_t, 0.0), axis=0, keepdims=True))
        qh = qc[:, sl]
        kh = kc[:, sl]
        qs.append(qh * (lax.rsqrt(jnp.sum(qh * qh, axis=-1, keepdims=True) + L2_EPS) * (dk ** -0.5)))
        ks.append(kh * lax.rsqrt(jnp.sum(kh * kh, axis=-1, keepdims=True) + L2_EPS))
        vs.append(vc[:, sl])

    lane1 = _iota2((1, LANE), 1)
    l_ps, a16s = [], []
    for pr in range(hb // 2):
        ha, hbb = 2 * pr, 2 * pr + 1
        gc_p = jnp.where(first, gcs[ha], gcs[hbb])
        beta_p = jnp.where(first, betas[ha], betas[hbb])
        gr_rows = []
        for c in range(nchunk):
            vsl = slice((c // 2) * LANE, (c // 2 + 1) * LANE)
            ra, rb = grows[ha][:, vsl], grows[hbb][:, vsl]
            if c % 2 == 0:
                rb = pltpu.roll(rb, CHUNK, 1)
            else:
                ra = pltpu.roll(ra, CHUNK, 1)
            gr_rows.append(jnp.broadcast_to(jnp.where(lane1 < CHUNK, ra, rb), (CHUNK, LANE)))
        gr_p = jnp.concatenate(gr_rows, axis=0)
        decay_p = jnp.exp(jnp.where(rloc >= cloc, gc_p - gr_p, -jnp.inf))
        q16 = jnp.concatenate([qs[ha], qs[hbb]], axis=1).astype(BF16)
        k16 = jnp.concatenate([ks[ha], ks[hbb]], axis=1).astype(BF16)
        rr = [_dot_nt(jnp.concatenate([q16[c * CHUNK:(c + 1) * CHUNK], k16[c * CHUNK:(c + 1) * CHUNK]], axis=0),
                      _pair_block_diag(k16[c * CHUNK:(c + 1) * CHUNK], dk)) for c in range(nchunk)]
        qk_p = jnp.concatenate([r[:CHUNK] for r in rr], axis=0)
        kk_p = jnp.concatenate([r[CHUNK:] for r in rr], axis=0)
        l_ps.append(jnp.where(rloc > cloc, kk_p * beta_p * decay_p, 0.0))
        a16s.append((qk_p * decay_p).astype(BF16))
    tinv = _pair_unit_lower_inverse(l_ps, sel_ref[...], nchunk)

    rhs16, kds, qds, egl = [], [], [], []
    for hh in range(hb):
        eg = jnp.exp(gcs[hh])
        rhs16.append(jnp.concatenate([vs[hh] * betas[hh], ks[hh] * (betas[hh] * eg)], axis=1).astype(BF16))
        g_last = jnp.concatenate(
            [jnp.broadcast_to(gcs[hh][(c + 1) * CHUNK - 1:(c + 1) * CHUNK, :], (CHUNK, 1)) for c in range(nchunk)],
            axis=0)
        kds.append(ks[hh] * jnp.exp(g_last - gcs[hh]))
        qds.append(qs[hh] * eg)
        egl.append(jnp.exp(g_last))
    zeros_rhs = jnp.zeros((CHUNK, 2 * dk), BF16)

    def pair_bd(left, right):
        return jnp.concatenate([jnp.concatenate([left, zeros_rhs], axis=1),
                                jnp.concatenate([zeros_rhs, right], axis=1)], axis=0)

    pcs = [(pr, c) for pr in range(hb // 2) for c in range(nchunk)]
    rows = lambda c: slice(c * CHUNK, (c + 1) * CHUNK)
    uw16 = [_dot(tinv[i].astype(BF16), pair_bd(rhs16[2 * pr][rows(c)], rhs16[2 * pr + 1][rows(c)])).astype(BF16)
            for i, (pr, c) in enumerate(pcs)]
    a_uw = [_dot(a16s[pr][rows(c)], pair_bd(uw16[i][:, :2 * dk], uw16[i][:, 2 * dk:]))
            for i, (pr, c) in enumerate(pcs)]
    for i, (pr, c) in enumerate(pcs):
        for side in range(2):
            hh = 2 * pr + side
            off = side * 2 * dk
            k_uw = _dot(kds[hh][rows(c)].T.astype(BF16), uw16[i][:, off:off + 2 * dk])
            pq_ref[hh, c, 0:dk, :] = k_uw[:, dk:].astype(BF16)
            pq_ref[hh, c, dk:dk + CHUNK, :] = (qds[hh][rows(c)] - a_uw[i][:, off + dk:off + 2 * dk]).astype(BF16)
            n_ref[hh, c] = k_uw[:, :dk]
            oc_ref[hh, c] = a_uw[i][:, off:off + dk]
            gl_ref[hh, c] = jnp.broadcast_to(egl[hh][(c + 1) * CHUNK - 1:(c + 1) * CHUNK, :], (8, LANE))

    def scan(c, carry):
        for hh in range(hb):
            s = s_ref[hh]
            res = _dot(pq_ref[hh, c], s.astype(BF16))
            s_ref[hh] = s * gl_ref[hh, c, 0:1, :] - res[:dk] + n_ref[hh, c]
            oc_ref[hh, c] = res[dk:] + oc_ref[hh, c]
        return carry

    lax.fori_loop(0, nchunk, scan, 0)

    z = _dot(xb_ref[...], w_ref[0, :, 3 * w:4 * w])
    for hh in range(hb):
        sl = slice(hh * dk, (hh + 1) * dk)
        o = oc_ref[hh].reshape(tc, dk)
        o_ref[0, :, sl] = _gated_rms_norm(o, z[:, sl], nw_ref[...]).astype(o_ref.dtype)


def _gdn_cj_selector():
    w = 2 * CHUNK
    i = jnp.arange(w)[:, None]
    c = jnp.arange(w)[None, :]
    blocks = [((i // CHUNK) == (c // CHUNK)) & ((i % GSUB) == j) for j in range(GSUB - 1)]
    return jnp.concatenate(blocks, axis=1).astype(BF16)


GDN_HB = 4


def _gdn_mixer(x3, w_grp, w_side, conv_w, par, norm_w, hb=GDN_HB, tc=256):
    batch, seq, d = x3.shape
    dk = LANE
    assert hb % 2 == 0 and GDN_HEADS % hb == 0
    w = hb * dk
    nhb = GDN_HEADS // hb
    nchunk = tc // CHUNK
    cwcol = lambda off: (lambda b, h, t: (0, off * nhb + h))
    sel = _gdn_cj_selector()
    kern = functools.partial(_gdn_kernel, hb=hb, tc=tc)
    return pl.pallas_call(
        kern,
        grid=(batch, nhb, seq // tc),
        in_specs=[pl.BlockSpec((1, tc, d), lambda b, h, t: (b, t, 0)),
                  pl.BlockSpec((1, d, 4 * w), lambda b, h, t: (h, 0, 0)),
                  pl.BlockSpec((d, LANE), lambda b, h, t: (0, 0)),
                  pl.BlockSpec((4, w), cwcol(0)),
                  pl.BlockSpec((4, w), cwcol(1)),
                  pl.BlockSpec((4, w), cwcol(2)),
                  pl.BlockSpec((2, LANE), lambda b, h, t: (0, 0)),
                  pl.BlockSpec((1, dk), lambda b, h, t: (0, 0)),
                  pl.BlockSpec(sel.shape, lambda b, h, t: (0, 0))],
        out_specs=pl.BlockSpec((1, tc, w), lambda b, h, t: (b, t, h)),
        out_shape=jax.ShapeDtypeStruct((batch, seq, GDN_HEADS * dk), BF16),
        scratch_shapes=[pltpu.VMEM((hb, dk, dk), F32),
                        pltpu.VMEM((tc, d), BF16),
                        pltpu.VMEM((tc, w), F32),
                        pltpu.VMEM((tc, w), F32),
                        pltpu.VMEM((tc, w), F32),
                        pltpu.VMEM((16, w), F32),
                        pltpu.VMEM((16, w), F32),
                        pltpu.VMEM((16, w), F32),
                        pltpu.VMEM((hb, nchunk, dk + CHUNK, dk), BF16),
                        pltpu.VMEM((hb, nchunk, dk, dk), F32),
                        pltpu.VMEM((hb, nchunk, CHUNK, dk), F32),
                        pltpu.VMEM((hb, nchunk, 8, LANE), F32)],
        compiler_params=_cparams(("parallel", "parallel", "arbitrary")),
        name="gdn_mixer",
    )(x3, w_grp, w_side, conv_w, conv_w, conv_w, par, norm_w, sel)


def _gla_pair_scores(q, k, big, nchunk):
    n, wk = q.shape
    dk = wk // 2
    row = _iota2((n, wk), 0)
    rl = _iota2((n, 2 * CHUNK), 0) & (CHUNK - 1)
    lane = _iota2((n, 2 * CHUNK), 1)
    cl = lane & (CHUNK - 1)
    a_p = jnp.zeros((n, 2 * CHUNK), F32)

    w = CHUNK // 2
    shift = CHUNK_SHIFT - 1
    while w >= SUB:
        gb = jnp.broadcast_to(big.reshape(n // (2 * w), 2 * w, wk)[:, w - 1:w, :], (n // (2 * w), 2 * w, wk))
        gb = gb.reshape(n, wk)
        upper = (row & (2 * w - 1)) >= w
        e = jnp.exp(jnp.where(upper, big - gb, gb - big))
        qe = (q * e).astype(BF16)
        ke = (k * e).astype(BF16)
        lvl = jnp.concatenate(
            [_dot_nt(qe[c * CHUNK:(c + 1) * CHUNK], _pair_block_diag(ke[c * CHUNK:(c + 1) * CHUNK], dk))
             for c in range(nchunk)], axis=0)
        valid = ((rl & (2 * w - 1)) >= w) & ((cl & (2 * w - 1)) < w) & ((rl >> (shift + 1)) == (cl >> (shift + 1)))
        a_p = jnp.where(valid, lvl, a_p)
        w //= 2
        shift -= 1

    rsub = row & (SUB - 1)
    for d in range(SUB):
        if d == 0:
            p = q * k
        else:
            kd = pltpu.roll(k, d, 0)
            gd = pltpu.roll(big, d, 0)
            p = q * kd * jnp.exp(jnp.where(rsub >= d, big - gd, -jnp.inf))
        diag_a = jnp.sum(p[:, :dk], axis=-1, keepdims=True)
        diag_b = jnp.sum(p[:, dk:], axis=-1, keepdims=True)
        on_diag = (cl == rl - d) & ((rl & (SUB - 1)) >= d)
        a_p = jnp.where(on_diag, jnp.where(lane < CHUNK, diag_a, diag_b), a_p)
    return a_p


def _gla_kernel(x_ref, w_ref, aux_ref, w2_ref, b2_ref, nw_ref,
                o_ref,
                s_ref, xb_ref, *, tc, dv, mode, layer):
    t = pl.program_id(2)
    dk = LANE
    hb = 2
    nchunk = tc // CHUNK

    @pl.when(t == 0)
    def _():
        s_ref[...] = jnp.zeros_like(s_ref)

    wk = hb * dk
    wv = hb * dv
    xb_ref[...] = x_ref[0].astype(BF16)
    q_raw = _dot(xb_ref[...], w_ref[0, :, 0:wk])
    k_raw = _dot(xb_ref[...], w_ref[0, :, wk:2 * wk])
    v16 = _dot(xb_ref[...], w_ref[0, :, 2 * wk:2 * wk + wv]).astype(BF16)

    if mode == "hgrn":
        lg = aux_ref[...]
        e = jnp.exp(lg - jnp.max(lg, axis=0, keepdims=True))
        soft = e / jnp.sum(e, axis=0, keepdims=True)
        lb = jnp.zeros((1, hb * dk), F32)
        for j in range(1, layer + 1):
            lb = lb + soft[j:j + 1, :]
        f_raw = k_raw
        et = jnp.exp(-jnp.abs(f_raw))
        bterm = jnp.log1p(-lb) + (jnp.minimum(f_raw, 0.0) - _log1p_unit(et))
        log_lb = jnp.log(lb)
        g = jnp.maximum(log_lb, bterm) + _log1p_unit(jnp.exp(-jnp.abs(log_lb - bterm)))
        k = (1.0 - lb) * (jnp.where(f_raw >= 0.0, et, 1.0) / (1.0 + et))
        q = _silu(q_raw) * (dk ** -0.5)
    else:
        gk = _dot(xb_ref[...], aux_ref[...])
        x = _dot(gk, w2_ref[...], HI) + b2_ref[...]
        g = _log_sigmoid(x) / GLA_TAU
        k = k_raw
        q = q_raw * (dk ** -0.5)

    big = _chunk_cumsum(g, tc)
    g_last = jnp.broadcast_to(big.reshape(nchunk, CHUNK, hb * dk)[:, CHUNK - 1:CHUNK, :],
                              (nchunk, CHUNK, hb * dk)).reshape(tc, hb * dk)
    q_dec = (q * jnp.exp(big)).astype(BF16)
    k_dec = (k * jnp.exp(g_last - big)).astype(BF16)
    chunk_decay = jnp.exp(g_last)
    a16 = _gla_pair_scores(q, k, big, nchunk).astype(BF16)

    zeros_v = jnp.zeros((CHUNK, dv), BF16)
    states = [s_ref[hh] for hh in range(hb)]
    outs = [[] for _ in range(hb)]
    for c in range(nchunk):
        rs = slice(c * CHUNK, (c + 1) * CHUNK)
        v_bd = jnp.concatenate([jnp.concatenate([v16[rs, :dv], zeros_v], axis=1),
                                jnp.concatenate([zeros_v, v16[rs, dv:]], axis=1)], axis=0)
        intra = _dot(a16[rs], v_bd)
        for hh in range(hb):
            sl = slice(hh * dk, (hh + 1) * dk)
            slv = slice(hh * dv, (hh + 1) * dv)
            s = states[hh]
            outs[hh].append(_dot_nt(q_dec[rs, sl], s.astype(BF16)) + intra[:, slv])
            kv = lax.dot_general(v16[rs, slv], k_dec[rs, sl], (((0,), (0,)), ((), ())), preferred_element_type=F32)
            states[hh] = s * chunk_decay[(c + 1) * CHUNK - 1:(c + 1) * CHUNK, sl] + kv
    r_gate = _dot(xb_ref[...], w_ref[0, :, 2 * wk + wv:2 * wk + 2 * wv])
    for hh in range(hb):
        slv = slice(hh * dv, (hh + 1) * dv)
        s_ref[hh] = states[hh]
        o = jnp.concatenate(outs[hh], axis=0)
        o_ref[0, :, slv] = _gated_rms_norm(o, r_gate[:, slv], nw_ref[...]).astype(o_ref.dtype)


def _gla_mixer(x3, w_grp, aux, w2, b2, norm_w, *, mode, layer, heads, dv, tc=256):
    batch, seq, d = x3.shape
    dk = LANE
    hb = 2
    wk = hb * dk
    wv = hb * dv
    nhb = heads // hb
    kern = functools.partial(_gla_kernel, tc=tc, dv=dv, mode=mode, layer=layer)
    if mode == "hgrn":
        aux_spec = pl.BlockSpec((DEPTH, wk), lambda b, h, t: (0, h))
    else:
        aux_spec = pl.BlockSpec((d, LANE), lambda b, h, t: (0, 0))
    return pl.pallas_call(
        kern,
        grid=(batch, nhb, seq // tc),
        in_specs=[pl.BlockSpec((1, tc, d), lambda b, h, t: (b, t, 0)),
                  pl.BlockSpec((1, d, 2 * wk + 2 * wv), lambda b, h, t: (h, 0, 0)),
                  aux_spec,
                  pl.BlockSpec((LANE, wk), lambda b, h, t: (0, h)),
                  pl.BlockSpec((1, wk), lambda b, h, t: (0, h)),
                  pl.BlockSpec((1, dv), lambda b, h, t: (0, 0))],
        out_specs=pl.BlockSpec((1, tc, wv), lambda b, h, t: (b, t, h)),
        out_shape=jax.ShapeDtypeStruct((batch, seq, heads * dv), BF16),
        scratch_shapes=[pltpu.VMEM((hb, dv, dk), F32),
                        pltpu.VMEM((tc, d), BF16)],
        compiler_params=_cparams(("parallel", "parallel", "arbitrary")),
        name=mode + "_mixer",
    )(x3, w_grp, aux, w2, b2, norm_w)


def _pad_cols(w, width=LANE):
    return jnp.pad(w, ((0, 0), (0, width - w.shape[1])))


def _group_columns(w_in, parts, ngroups):
    groups = []
    for g in range(ngroups):
        groups.append(jnp.concatenate(
            [w_in[:, s + g * (wd // ngroups):s + (g + 1) * (wd // ngroups)] for s, wd in parts], axis=1))
    return jnp.stack(groups).astype(BF16)


def kernel(x, gdn_w_in, gdn_conv_w, gdn_a_log, gdn_dt_bias, gdn_norm_w, gdn_w_out, hgrn_w_in, hgrn_lb_logits, hgrn_norm_w, hgrn_w_out, gla_w_in, gla_w_gk2, gla_b_gk, gla_norm_w, gla_w_out, mix_ln_g, mix_ln_b, ffn_w_in, ffn_w_down, ffn_ln_g, ffn_ln_b):
    batch, seq, d = x.shape
    n = batch * seq
    x2 = x.reshape(n, d)
    for i in range(DEPTH):
        kind, j = i % 3, i // 3
        x3 = x2.reshape(batch, seq, d)
        if kind == 0:
            w_in = gdn_w_in[j]
            w_grp = _group_columns(w_in, [(0, 1024), (1024, 1024), (2048, 1024), (3072, 1024)], GDN_HEADS // GDN_HB)
            par = jnp.zeros((2, LANE), F32)
            par = par.at[0, GDN_HEADS:2 * GDN_HEADS].set(gdn_a_log[j]).at[1, GDN_HEADS:2 * GDN_HEADS].set(gdn_dt_bias[j])
            o = _gdn_mixer(x3, w_grp, _pad_cols(w_in[:, 4096:]).astype(BF16), gdn_conv_w[j], par,
                           gdn_norm_w[j].reshape(1, -1))
            w_out = gdn_w_out[j]
        elif kind == 1:
            w_grp = _group_columns(hgrn_w_in[j], [(0, 1024), (1024, 1024), (2048, 1024), (3072, 1024)], 4)
            o = _gla_mixer(x3, w_grp, hgrn_lb_logits, jnp.zeros((LANE, D_MODEL), F32),
                           jnp.zeros((1, D_MODEL), F32), hgrn_norm_w[j].reshape(1, -1),
                           mode="hgrn", layer=i, heads=8, dv=128)
            w_out = hgrn_w_out[j]
        else:
            w_in = gla_w_in[j]
            w_grp = _group_columns(w_in, [(0, 512), (512, 512), (1024, 1024), (2048, 1024)], 2)
            w2 = jnp.pad(gla_w_gk2[j], ((0, LANE - gla_w_gk2.shape[1]), (0, 0)))
            o = _gla_mixer(x3, w_grp, _pad_cols(w_in[:, 3072:]).astype(BF16), w2,
                           gla_b_gk[j].reshape(1, -1), gla_norm_w[j].reshape(1, -1),
                           mode="gla", layer=i, heads=4, dv=256)
            w_out = gla_w_out[j]
        x2 = _outproj_ln(o.reshape(n, -1), w_out.astype(BF16), x2, mix_ln_g[i].reshape(1, -1), mix_ln_b[i].reshape(1, -1))
        x2 = _ffn(x2, ffn_w_in[i].astype(BF16), ffn_w_down[i].astype(BF16), ffn_ln_g[i].reshape(1, -1), ffn_ln_b[i].reshape(1, -1))
    return x2.reshape(batch, seq, d)
```

```python
import functools
import math

import jax
import jax.numpy as jnp
from jax import lax
from jax.experimental import pallas as pl
from jax.experimental.pallas import tpu as pltpu

F32 = jnp.float32
BF16 = jnp.bfloat16
HI = lax.Precision.HIGHEST

D_MODEL = 1024
DEPTH = 4
CHUNK = 64
CHUNK_SHIFT = 6
SUB = 4
GSUB = 8
GSUB_SHIFT = 3
LN_EPS = 1e-5
RMS_EPS = 1e-6
L2_EPS = 1e-6
DN_ALPHA = (2 * DEPTH) ** 0.25
D_FF = 2816
LANE = 128
GDN_HEADS = 8
PROJ_PIECE = 256
GLA_TAU = 16.0

VMEM_LIMIT = 56 * 1024 * 1024


def _cparams(sem):
    return pltpu.CompilerParams(dimension_semantics=sem, vmem_limit_bytes=VMEM_LIMIT)


def _dot(a, b, precision=None):
    return jnp.dot(a, b, preferred_element_type=F32, precision=precision)


def _dot_nt(a, b, precision=None):
    return lax.dot_general(a, b, (((1,), (1,)), ((), ())), preferred_element_type=F32, precision=precision)


def _sigmoid(x):
    return 1.0 / (1.0 + jnp.exp(-x))


def _silu(x):
    return x * _sigmoid(x)


def _log1p_unit(x):
    return jnp.log(1.0 + x)


def _log_sigmoid(x):
    return jnp.minimum(x, 0.0) - _log1p_unit(jnp.exp(-jnp.abs(x)))


def _softplus(x):
    return jnp.maximum(x, 0.0) + _log1p_unit(jnp.exp(-jnp.abs(x)))


def _chunk_cumsum(g, tc):
    rt = _iota2((tc, tc), 0)
    ct = _iota2((tc, tc), 1)
    tri = ((rt >= ct) & ((rt >> CHUNK_SHIFT) == (ct >> CHUNK_SHIFT))).astype(BF16)
    hi = g.astype(BF16)
    r1 = g - hi.astype(F32)
    mid = r1.astype(BF16)
    lo = (r1 - mid.astype(F32)).astype(BF16)
    return _dot(tri, hi) + _dot(tri, mid) + _dot(tri, lo)


def _layer_norm(h, g, b):
    mu = jnp.mean(h, axis=-1, keepdims=True)
    d = h - mu
    var = jnp.mean(d * d, axis=-1, keepdims=True)
    return d * lax.rsqrt(var + LN_EPS) * g + b


def _iota2(shape, dim):
    return lax.broadcasted_iota(jnp.int32, shape, dim)


def _tril_ones(n):
    return (_iota2((n, n), 0) >= _iota2((n, n), 1)).astype(F32)


def _outproj_ffn_kernel(o_ref, wo_ref, x_ref, g1_ref, b1_ref, wg_ref, wu_ref, wd_ref, g2_ref, b2_ref,
                        y_ref, x1_ref, xb_ref, acc_ref):
    f = pl.program_id(1)

    @pl.when(f == 0)
    def _():
        x1 = _layer_norm(DN_ALPHA * x_ref[...] + _dot(o_ref[...], wo_ref[...]), g1_ref[...], b1_ref[...])
        x1_ref[...] = x1
        xb_ref[...] = x1.astype(BF16)
        acc_ref[...] = jnp.zeros_like(acc_ref)

    xb = xb_ref[...]
    gate = _dot(xb, wg_ref[...])
    up = _dot(xb, wu_ref[...])
    h = (_silu(gate) * up).astype(BF16)
    acc_ref[...] += _dot(h, wd_ref[...])

    @pl.when(f == pl.num_programs(1) - 1)
    def _():
        y_ref[...] = _layer_norm(DN_ALPHA * x1_ref[...] + acc_ref[...], g2_ref[...], b2_ref[...])


def _outproj_ffn(o2, w_out, x2, g1, b1, w_in, w_down, g2, b2, tm=1024, tf=256):
    n, d = x2.shape
    k = o2.shape[1]
    nf = D_FF // tf
    row = lambda i, f: (i, 0)
    fixed = lambda i, f: (0, 0)
    return pl.pallas_call(
        _outproj_ffn_kernel,
        grid=(n // tm, nf),
        in_specs=[pl.BlockSpec((tm, k), row),
                  pl.BlockSpec((k, d), fixed),
                  pl.BlockSpec((tm, d), row),
                  pl.BlockSpec((1, d), fixed),
                  pl.BlockSpec((1, d), fixed),
                  pl.BlockSpec((d, tf), lambda i, f: (0, f)),
                  pl.BlockSpec((d, tf), lambda i, f: (0, f + nf)),
                  pl.BlockSpec((tf, d), lambda i, f: (f, 0)),
                  pl.BlockSpec((1, d), fixed),
                  pl.BlockSpec((1, d), fixed)],
        out_specs=pl.BlockSpec((tm, d), row),
        out_shape=jax.ShapeDtypeStruct((n, d), F32),
        scratch_shapes=[pltpu.VMEM((tm, d), F32),
                        pltpu.VMEM((tm, d), BF16),
                        pltpu.VMEM((tm, d), F32)],
        compiler_params=_cparams(("parallel", "arbitrary")),
        name="outproj_ffn",
    )(o2, w_out, x2, g1, b1, w_in, w_in, w_down, g2, b2)


def _gated_rms_norm(o, gate, w):
    y = o * lax.rsqrt(jnp.mean(o * o, axis=-1, keepdims=True) + RMS_EPS) * w
    return y * _silu(gate)


def _pair_block_diag(x, lane_split):
    lane = _iota2(x.shape, 1)
    zero = jnp.zeros_like(x)
    return jnp.concatenate([jnp.where(lane < lane_split, x, zero), jnp.where(lane >= lane_split, x, zero)], axis=0)


def _pair_unit_lower_inverse(l_ps, cj_sel, nchunk):
    n = nchunk * CHUNK
    w = 2 * CHUNK
    row = _iota2((n, w), 0) & (CHUNK - 1)
    col = _iota2((n, w), 1) & (CHUNK - 1)
    eye = (row == col).astype(F32).reshape(n // GSUB, GSUB, w)
    diag_blk = (row >> GSUB_SHIFT) == (col >> GSUB_SHIFT)
    cj_alls = [_dot(jnp.where(diag_blk, l_p, 0.0).astype(BF16), cj_sel) for l_p in l_ps]
    xs, ls = [], []
    for l_p, cj_all in zip(l_ps, cj_alls):
        x = eye
        for j in range(GSUB - 1):
            cj = cj_all[:, j * w:(j + 1) * w].reshape(n // GSUB, GSUB, w)
            x = x - cj * x[:, j:j + 1, :]
        x = x.reshape(n, w)
        for c in range(nchunk):
            xs.append(x[c * CHUNK:(c + 1) * CHUNK])
            ls.append(l_p[c * CHUNK:(c + 1) * CHUNK])
    r64 = _iota2((CHUNK, w), 0)
    c64 = _iota2((CHUNK, w), 1) & (CHUNK - 1)
    shift = GSUB_SHIFT
    while (1 << shift) < CHUNK:
        sel = ((r64 >> (shift + 1)) == (c64 >> (shift + 1))) & ((r64 >> shift) != (c64 >> shift))
        x16 = [x.astype(BF16) for x in xs]
        m1 = [_dot(jnp.where(sel, lc, 0.0).astype(BF16), _pair_block_diag(xb, CHUNK)).astype(BF16)
              for lc, xb in zip(ls, x16)]
        xs = [x - _dot(xb, _pair_block_diag(m, CHUNK)) for x, xb, m in zip(xs, x16, m1)]
        shift += 1
    return xs


def _gdn_kernel(x_ref, xn_ref, w_ref, ws_ref, cwq_ref, cwk_ref, cwv_ref, par_ref, nw_ref, sel_ref,
                o_ref,
                s_ref, xb_ref, nxt_ref, rq_ref, rk_ref, rv_ref, z_ref, wq_ref, wk_ref, wv_ref,
                pq_ref, n_ref, oc_ref, gl_ref, *, hb, tc):
    hblk = pl.program_id(1)
    t = pl.program_id(2)
    dk = LANE
    nchunk = tc // CHUNK

    @pl.when(t == 0)
    def _():
        s_ref[...] = jnp.zeros_like(s_ref)
        wq_ref[0:8, :] = jnp.zeros((8, hb * dk), F32)
        wk_ref[0:8, :] = jnp.zeros((8, hb * dk), F32)
        wv_ref[0:8, :] = jnp.zeros((8, hb * dk), F32)

    w = hb * dk

    pieces = [(lo, lo + PROJ_PIECE) for lo in range(0, 4 * w, PROJ_PIECE)] + [None]
    pending = list(pieces)

    def project_piece():
        if pending:
            piece = pending.pop(0)
            if piece is None:
                nxt_ref[:, 4 * w:4 * w + LANE] = _dot(xb_ref[...], ws_ref[...])
            else:
                nxt_ref[:, piece[0]:piece[1]] = _dot(xb_ref[...], w_ref[0, :, piece[0]:piece[1]])

    @pl.when(t == 0)
    def _():
        xb_ref[...] = x_ref[0].astype(BF16)
        for _ in pieces:
            project_piece()

    rq_ref[...] = nxt_ref[:, 0:w]
    rk_ref[...] = nxt_ref[:, w:2 * w]
    rv_ref[...] = nxt_ref[:, 2 * w:3 * w]
    z_ref[...] = nxt_ref[:, 3 * w:4 * w]
    ba = nxt_ref[:, 4 * w:4 * w + LANE]
    xb_ref[...] = xn_ref[0].astype(BF16)
    pending = list(pieces)

    def conv_silu(src_ref, win_ref, cw_ref):
        win_ref[8:16, :] = src_ref[0:8, :]
        head = cw_ref[3:4, :] * win_ref[8:16, :]
        body = cw_ref[3:4, :] * src_ref[8:tc, :]
        for j in range(3):
            head = head + cw_ref[j:j + 1, :] * win_ref[5 + j:13 + j, :]
            body = body + cw_ref[j:j + 1, :] * src_ref[5 + j:tc - 3 + j, :]
        win_ref[0:8, :] = src_ref[tc - 8:tc, :]
        return _silu(jnp.concatenate([head, body], axis=0))

    project_piece()
    project_piece()
    qc = conv_silu(rq_ref, wq_ref, cwq_ref)
    project_piece()
    project_piece()
    kc = conv_silu(rk_ref, wk_ref, cwk_ref)
    project_piece()
    project_piece()
    vc = conv_silu(rv_ref, wv_ref, cwv_ref)
    project_piece()

    beta_all = _sigmoid(ba)
    g_all = -jnp.exp(par_ref[0:1, :]) * _softplus(ba + par_ref[1:2, :])
    big_all = _chunk_cumsum(g_all, tc)
    big_all_t = big_all.T

    lane = _iota2((tc, LANE), 1)
    subl = _iota2((LANE, tc), 0)
    first = lane < CHUNK
    rloc = _iota2((tc, LANE), 0) & (CHUNK - 1)
    cloc = lane & (CHUNK - 1)

    gcs, betas, grows, qs, ks, vs = [], [], [], [], [], []
    for hh in range(hb):
        sl = slice(hh * dk, (hh + 1) * dk)
        h = hblk * hb + hh
        gcs.append(jnp.sum(jnp.where(lane == GDN_HEADS + h, big_all, 0.0), axis=-1, keepdims=True))
        betas.append(jnp.sum(jnp.where(lane == h, beta_all, 0.0), axis=-1, keepdims=True))
        grows.append(jnp.sum(jnp.where(subl == GDN_HEADS + h, big_all_t, 0.0), axis=0, keepdims=True))
        qh = qc[:, sl]
        kh = kc[:, sl]
        qs.append(qh * (lax.rsqrt(jnp.sum(qh * qh, axis=-1, keepdims=True) + L2_EPS) * (dk ** -0.5)))
        ks.append(kh * lax.rsqrt(jnp.sum(kh * kh, axis=-1, keepdims=True) + L2_EPS))
        vs.append(vc[:, sl])
    while pending:
        project_piece()

    lane1 = _iota2((1, LANE), 1)
    l_ps, a16s = [], []
    for pr in range(hb // 2):
        ha, hbb = 2 * pr, 2 * pr + 1
        gc_p = jnp.where(first, gcs[ha], gcs[hbb])
        beta_p = jnp.where(first, betas[ha], betas[hbb])
        gr_rows = []
        for c in range(nchunk):
            vsl = slice((c // 2) * LANE, (c // 2 + 1) * LANE)
            ra, rb = grows[ha][:, vsl], grows[hbb][:, vsl]
            if c % 2 == 0:
                rb = pltpu.roll(rb, CHUNK, 1)
            else:
                ra = pltpu.roll(ra, CHUNK, 1)
            gr_rows.append(jnp.broadcast_to(jnp.where(lane1 < CHUNK, ra, rb), (CHUNK, LANE)))
        gr_p = jnp.concatenate(gr_rows, axis=0)
        decay_p = jnp.exp(jnp.where(rloc >= cloc, gc_p - gr_p, -jnp.inf))
        q16 = jnp.concatenate([qs[ha], qs[hbb]], axis=1).astype(BF16)
        k16 = jnp.concatenate([ks[ha], ks[hbb]], axis=1).astype(BF16)
        rr = [_dot_nt(jnp.concatenate([q16[c * CHUNK:(c + 1) * CHUNK], k16[c * CHUNK:(c + 1) * CHUNK]], axis=0),
                      _pair_block_diag(k16[c * CHUNK:(c + 1) * CHUNK], dk)) for c in range(nchunk)]
        qk_p = jnp.concatenate([r[:CHUNK] for r in rr], axis=0)
        kk_p = jnp.concatenate([r[CHUNK:] for r in rr], axis=0)
        l_ps.append(jnp.where(rloc > cloc, kk_p * beta_p * decay_p, 0.0))
        a16s.append((qk_p * decay_p).astype(BF16))
    tinv = _pair_unit_lower_inverse(l_ps, sel_ref[...], nchunk)

    rhs16, kds, qds, egl = [], [], [], []
    for hh in range(hb):
        eg = jnp.exp(gcs[hh])
        rhs16.append(jnp.concatenate([vs[hh] * betas[hh], ks[hh] * (betas[hh] * eg)], axis=1).astype(BF16))
        g_last = jnp.concatenate(
            [jnp.broadcast_to(gcs[hh][(c + 1) * CHUNK - 1:(c + 1) * CHUNK, :], (CHUNK, 1)) for c in range(nchunk)],
            axis=0)
        kds.append(ks[hh] * jnp.exp(g_last - gcs[hh]))
        qds.append(qs[hh] * eg)
        egl.append(jnp.exp(g_last))
    zeros_rhs = jnp.zeros((CHUNK, 2 * dk), BF16)

    def pair_bd(left, right):
        return jnp.concatenate([jnp.concatenate([left, zeros_rhs], axis=1),
                                jnp.concatenate([zeros_rhs, right], axis=1)], axis=0)

    pcs = [(pr, c) for pr in range(hb // 2) for c in range(nchunk)]
    rows = lambda c: slice(c * CHUNK, (c + 1) * CHUNK)
    uw16 = [_dot(tinv[i].astype(BF16), pair_bd(rhs16[2 * pr][rows(c)], rhs16[2 * pr + 1][rows(c)])).astype(BF16)
            for i, (pr, c) in enumerate(pcs)]
    a_uw = [_dot(a16s[pr][rows(c)], pair_bd(uw16[i][:, :2 * dk], uw16[i][:, 2 * dk:]))
            for i, (pr, c) in enumerate(pcs)]
    for i, (pr, c) in enumerate(pcs):
        for side in range(2):
            hh = 2 * pr + side
            off = side * 2 * dk
            k_uw = _dot(kds[hh][rows(c)].T.astype(BF16), uw16[i][:, off:off + 2 * dk])
            pq_ref[hh, c, 0:dk, :] = k_uw[:, dk:].astype(BF16)
            pq_ref[hh, c, dk:dk + CHUNK, :] = (qds[hh][rows(c)] - a_uw[i][:, off + dk:off + 2 * dk]).astype(BF16)
            n_ref[hh, c] = k_uw[:, :dk]
            oc_ref[hh, c] = a_uw[i][:, off:off + dk]
            gl_ref[hh, c] = jnp.broadcast_to(egl[hh][(c + 1) * CHUNK - 1:(c + 1) * CHUNK, :], (8, LANE))

    def scan(c, carry):
        for hh in range(hb):
            s = s_ref[hh]
            res = _dot(pq_ref[hh, c], s.astype(BF16))
            s_ref[hh] = s * gl_ref[hh, c, 0:1, :] - res[:dk] + n_ref[hh, c]
            oc_ref[hh, c] = res[dk:] + oc_ref[hh, c]
        return carry

    lax.fori_loop(0, nchunk, scan, 0)

    for hh in range(hb):
        sl = slice(hh * dk, (hh + 1) * dk)
        o = oc_ref[hh].reshape(tc, dk)
        o_ref[0, :, sl] = _gated_rms_norm(o, z_ref[:, sl], nw_ref[...]).astype(o_ref.dtype)


def _gdn_cj_selector():
    w = 2 * CHUNK
    i = jnp.arange(w)[:, None]
    c = jnp.arange(w)[None, :]
    blocks = [((i // CHUNK) == (c // CHUNK)) & ((i % GSUB) == j) for j in range(GSUB - 1)]
    return jnp.concatenate(blocks, axis=1).astype(BF16)


GDN_HB = 4


def _gdn_mixer(x3, w_grp, w_side, conv_w, par, norm_w, hb=GDN_HB, tc=256):
    batch, seq, d = x3.shape
    dk = LANE
    assert hb % 2 == 0 and GDN_HEADS % hb == 0
    w = hb * dk
    nhb = GDN_HEADS // hb
    nchunk = tc // CHUNK
    nt = seq // tc
    cwcol = lambda off: (lambda b, h, t: (0, off * nhb + h))
    sel = _gdn_cj_selector()
    kern = functools.partial(_gdn_kernel, hb=hb, tc=tc)
    return pl.pallas_call(
        kern,
        grid=(batch, nhb, seq // tc),
        in_specs=[pl.BlockSpec((1, tc, d), lambda b, h, t: (b, 0, 0)),
                  pl.BlockSpec((1, tc, d), lambda b, h, t: (b, jnp.minimum(t + 1, nt - 1), 0)),
                  pl.BlockSpec((1, d, 4 * w), lambda b, h, t: (h, 0, 0)),
                  pl.BlockSpec((d, LANE), lambda b, h, t: (0, 0)),
                  pl.BlockSpec((4, w), cwcol(0)),
                  pl.BlockSpec((4, w), cwcol(1)),
                  pl.BlockSpec((4, w), cwcol(2)),
                  pl.BlockSpec((2, LANE), lambda b, h, t: (0, 0)),
                  pl.BlockSpec((1, dk), lambda b, h, t: (0, 0)),
                  pl.BlockSpec(sel.shape, lambda b, h, t: (0, 0))],
        out_specs=pl.BlockSpec((1, tc, w), lambda b, h, t: (b, t, h)),
        out_shape=jax.ShapeDtypeStruct((batch, seq, GDN_HEADS * dk), BF16),
        scratch_shapes=[pltpu.VMEM((hb, dk, dk), F32),
                        pltpu.VMEM((tc, d), BF16),
                        pltpu.VMEM((tc, 4 * w + LANE), F32),
                        pltpu.VMEM((tc, w), F32),
                        pltpu.VMEM((tc, w), F32),
                        pltpu.VMEM((tc, w), F32),
                        pltpu.VMEM((tc, w), F32),
                        pltpu.VMEM((16, w), F32),
                        pltpu.VMEM((16, w), F32),
                        pltpu.VMEM((16, w), F32),
                        pltpu.VMEM((hb, nchunk, dk + CHUNK, dk), BF16),
                        pltpu.VMEM((hb, nchunk, dk, dk), F32),
                        pltpu.VMEM((hb, nchunk, CHUNK, dk), F32),
                        pltpu.VMEM((hb, nchunk, 8, LANE), F32)],
        compiler_params=_cparams(("parallel", "parallel", "arbitrary")),
        name="gdn_mixer",
    )(x3, x3, w_grp, w_side, conv_w, conv_w, conv_w, par, norm_w, sel)


def _gla_pair_scores(q, k, big, nchunk, between_stages):
    n, wk = q.shape
    dk = wk // 2
    row = _iota2((n, wk), 0)
    rl = _iota2((n, 2 * CHUNK), 0) & (CHUNK - 1)
    lane = _iota2((n, 2 * CHUNK), 1)
    cl = lane & (CHUNK - 1)
    a_p = jnp.zeros((n, 2 * CHUNK), F32)

    w = CHUNK // 2
    shift = CHUNK_SHIFT - 1
    while w >= SUB:
        gb = jnp.broadcast_to(big.reshape(n // (2 * w), 2 * w, wk)[:, w - 1:w, :], (n // (2 * w), 2 * w, wk))
        gb = gb.reshape(n, wk)
        upper = (row & (2 * w - 1)) >= w
        e = jnp.exp(jnp.where(upper, big - gb, gb - big))
        qe = (q * e).astype(BF16)
        ke = (k * e).astype(BF16)
        between_stages()
        lvl = jnp.concatenate(
            [_dot_nt(qe[c * CHUNK:(c + 1) * CHUNK], _pair_block_diag(ke[c * CHUNK:(c + 1) * CHUNK], dk))
             for c in range(nchunk)], axis=0)
        valid = ((rl & (2 * w - 1)) >= w) & ((cl & (2 * w - 1)) < w) & ((rl >> (shift + 1)) == (cl >> (shift + 1)))
        a_p = jnp.where(valid, lvl, a_p)
        w //= 2
        shift -= 1

    rsub = row & (SUB - 1)
    for d in range(SUB):
        between_stages()
        if d == 0:
            p = q * k
        else:
            kd = pltpu.roll(k, d, 0)
            gd = pltpu.roll(big, d, 0)
            p = q * kd * jnp.exp(jnp.where(rsub >= d, big - gd, -jnp.inf))
        diag_a = jnp.sum(p[:, :dk], axis=-1, keepdims=True)
        diag_b = jnp.sum(p[:, dk:], axis=-1, keepdims=True)
        on_diag = (cl == rl - d) & ((rl & (SUB - 1)) >= d)
        a_p = jnp.where(on_diag, jnp.where(lane < CHUNK, diag_a, diag_b), a_p)
    return a_p


def _gla_kernel(x_ref, xn_ref, w_ref, aux_ref, w2_ref, b2_ref, nw_ref,
                o_ref,
                s_ref, xb_ref, nxt_ref, cur_ref, *, tc, dv, mode, layer):
    t = pl.program_id(2)
    dk = LANE
    hb = 2
    nchunk = tc // CHUNK

    @pl.when(t == 0)
    def _():
        s_ref[...] = jnp.zeros_like(s_ref)

    wk = hb * dk
    wv = hb * dv
    wp = 2 * wk + 2 * wv

    pieces = [(lo, lo + PROJ_PIECE) for lo in range(0, wp, PROJ_PIECE)] + ([None] if mode == "gla" else [])
    pending = list(pieces)

    def project_piece():
        if pending:
            piece = pending.pop(0)
            if piece is None:
                nxt_ref[:, wp:wp + LANE] = _dot(xb_ref[...], aux_ref[...])
            else:
                nxt_ref[:, piece[0]:piece[1]] = _dot(xb_ref[...], w_ref[0, :, piece[0]:piece[1]])

    @pl.when(t == 0)
    def _():
        xb_ref[...] = x_ref[0].astype(BF16)
        for _ in pieces:
            project_piece()

    cur_ref[...] = nxt_ref[...]
    xb_ref[...] = xn_ref[0].astype(BF16)
    pending = list(pieces)
    project_piece()
    q_raw = cur_ref[:, 0:wk]
    k_raw = cur_ref[:, wk:2 * wk]
    v16 = cur_ref[:, 2 * wk:2 * wk + wv].astype(BF16)

    if mode == "hgrn":
        lg = aux_ref[...]
        e = jnp.exp(lg - jnp.max(lg, axis=0, keepdims=True))
        soft = e / jnp.sum(e, axis=0, keepdims=True)
        lb = jnp.zeros((1, hb * dk), F32)
        for j in range(1, layer + 1):
            lb = lb + soft[j:j + 1, :]
        f_raw = k_raw
        et = jnp.exp(-jnp.abs(f_raw))
        project_piece()
        bterm = jnp.log1p(-lb) + (jnp.minimum(f_raw, 0.0) - _log1p_unit(et))
        log_lb = jnp.log(lb)
        g = jnp.maximum(log_lb, bterm) + _log1p_unit(jnp.exp(-jnp.abs(log_lb - bterm)))
        k = (1.0 - lb) * (jnp.where(f_raw >= 0.0, et, 1.0) / (1.0 + et))
        q = _silu(q_raw) * (dk ** -0.5)
    else:
        gk = cur_ref[:, wp:wp + LANE]
        x = _dot(gk, w2_ref[...], HI) + b2_ref[...]
        g = _log_sigmoid(x) / GLA_TAU
        k = k_raw
        q = q_raw * (dk ** -0.5)

    big = _chunk_cumsum(g, tc)
    g_last = jnp.broadcast_to(big.reshape(nchunk, CHUNK, hb * dk)[:, CHUNK - 1:CHUNK, :],
                              (nchunk, CHUNK, hb * dk)).reshape(tc, hb * dk)
    q_dec = (q * jnp.exp(big)).astype(BF16)
    k_dec = (k * jnp.exp(g_last - big)).astype(BF16)
    chunk_decay = jnp.exp(g_last)
    a16 = _gla_pair_scores(q, k, big, nchunk, project_piece).astype(BF16)
    while pending:
        project_piece()

    zeros_v = jnp.zeros((CHUNK, dv), BF16)
    states = [s_ref[hh] for hh in range(hb)]
    outs = [[] for _ in range(hb)]
    for c in range(nchunk):
        rs = slice(c * CHUNK, (c + 1) * CHUNK)
        v_bd = jnp.concatenate([jnp.concatenate([v16[rs, :dv], zeros_v], axis=1),
                                jnp.concatenate([zeros_v, v16[rs, dv:]], axis=1)], axis=0)
        intra = _dot(a16[rs], v_bd)
        for hh in range(hb):
            sl = slice(hh * dk, (hh + 1) * dk)
            slv = slice(hh * dv, (hh + 1) * dv)
            s = states[hh]
            outs[hh].append(_dot_nt(q_dec[rs, sl], s.astype(BF16)) + intra[:, slv])
            kv = lax.dot_general(v16[rs, slv], k_dec[rs, sl], (((0,), (0,)), ((), ())), preferred_element_type=F32)
            states[hh] = s * chunk_decay[(c + 1) * CHUNK - 1:(c + 1) * CHUNK, sl] + kv
    r_gate = cur_ref[:, 2 * wk + wv:wp]
    for hh in range(hb):
        slv = slice(hh * dv, (hh + 1) * dv)
        s_ref[hh] = states[hh]
        o = jnp.concatenate(outs[hh], axis=0)
        o_ref[0, :, slv] = _gated_rms_norm(o, r_gate[:, slv], nw_ref[...]).astype(o_ref.dtype)


def _gla_mixer(x3, w_grp, aux, w2, b2, norm_w, *, mode, layer, heads, dv, tc=256):
    batch, seq, d = x3.shape
    dk = LANE
    hb = 2
    wk = hb * dk
    wv = hb * dv
    nhb = heads // hb
    nt = seq // tc
    wproj = 2 * wk + 2 * wv + (LANE if mode == "gla" else 0)
    kern = functools.partial(_gla_kernel, tc=tc, dv=dv, mode=mode, layer=layer)
    if mode == "hgrn":
        aux_spec = pl.BlockSpec((DEPTH, wk), lambda b, h, t: (0, h))
    else:
        aux_spec = pl.BlockSpec((d, LANE), lambda b, h, t: (0, 0))
    return pl.pallas_call(
        kern,
        grid=(batch, nhb, seq // tc),
        in_specs=[pl.BlockSpec((1, tc, d), lambda b, h, t: (b, 0, 0)),
                  pl.BlockSpec((1, tc, d), lambda b, h, t: (b, jnp.minimum(t + 1, nt - 1), 0)),
                  pl.BlockSpec((1, d, 2 * wk + 2 * wv), lambda b, h, t: (h, 0, 0)),
                  aux_spec,
                  pl.BlockSpec((LANE, wk), lambda b, h, t: (0, h)),
                  pl.BlockSpec((1, wk), lambda b, h, t: (0, h)),
                  pl.BlockSpec((1, dv), lambda b, h, t: (0, 0))],
        out_specs=pl.BlockSpec((1, tc, wv), lambda b, h, t: (b, t, h)),
        out_shape=jax.ShapeDtypeStruct((batch, seq, heads * dv), BF16),
        scratch_shapes=[pltpu.VMEM((hb, dv, dk), F32),
                        pltpu.VMEM((tc, d), BF16),
                        pltpu.VMEM((tc, wproj), F32),
                        pltpu.VMEM((tc, wproj), F32)],
        compiler_params=_cparams(("parallel", "parallel", "arbitrary")),
        name=mode + "_mixer",
    )(x3, x3, w_grp, aux, w2, b2, norm_w)


def _pad_cols(w, width=LANE):
    return jnp.pad(w, ((0, 0), (0, width - w.shape[1])))


def _group_columns(w_in, parts, ngroups):
    groups = []
    for g in range(ngroups):
        groups.append(jnp.concatenate(
            [w_in[:, s + g * (wd // ngroups):s + (g + 1) * (wd // ngroups)] for s, wd in parts], axis=1))
    return jnp.stack(groups).astype(BF16)


def kernel(x, gdn_w_in, gdn_conv_w, gdn_a_log, gdn_dt_bias, gdn_norm_w, gdn_w_out, hgrn_w_in, hgrn_lb_logits, hgrn_norm_w, hgrn_w_out, gla_w_in, gla_w_gk2, gla_b_gk, gla_norm_w, gla_w_out, mix_ln_g, mix_ln_b, ffn_w_in, ffn_w_down, ffn_ln_g, ffn_ln_b):
    batch, seq, d = x.shape
    n = batch * seq
    x2 = x.reshape(n, d)
    for i in range(DEPTH):
        kind, j = i % 3, i // 3
        x3 = x2.reshape(batch, seq, d)
        if kind == 0:
            w_in = gdn_w_in[j]
            w_grp = _group_columns(w_in, [(0, 1024), (1024, 1024), (2048, 1024), (3072, 1024)], GDN_HEADS // GDN_HB)
            par = jnp.zeros((2, LANE), F32)
            par = par.at[0, GDN_HEADS:2 * GDN_HEADS].set(gdn_a_log[j]).at[1, GDN_HEADS:2 * GDN_HEADS].set(gdn_dt_bias[j])
            o = _gdn_mixer(x3, w_grp, _pad_cols(w_in[:, 4096:]).astype(BF16), gdn_conv_w[j], par,
                           gdn_norm_w[j].reshape(1, -1))
            w_out = gdn_w_out[j]
        elif kind == 1:
            w_grp = _group_columns(hgrn_w_in[j], [(0, 1024), (1024, 1024), (2048, 1024), (3072, 1024)], 4)
            o = _gla_mixer(x3, w_grp, hgrn_lb_logits, jnp.zeros((LANE, D_MODEL), F32),
                           jnp.zeros((1, D_MODEL), F32), hgrn_norm_w[j].reshape(1, -1),
                           mode="hgrn", layer=i, heads=8, dv=128)
            w_out = hgrn_w_out[j]
        else:
            w_in = gla_w_in[j]
            w_grp = _group_columns(w_in, [(0, 512), (512, 512), (1024, 1024), (2048, 1024)], 2)
            w2 = jnp.pad(gla_w_gk2[j], ((0, LANE - gla_w_gk2.shape[1]), (0, 0)))
            o = _gla_mixer(x3, w_grp, _pad_cols(w_in[:, 3072:]).astype(BF16), w2,
                           gla_b_gk[j].reshape(1, -1), gla_norm_w[j].reshape(1, -1),
                           mode="gla", layer=i, heads=4, dv=256)
            w_out = gla_w_out[j]
        x2 = _outproj_ffn(o.reshape(n, -1), w_out.astype(BF16), x2, mix_ln_g[i].reshape(1, -1), mix_ln_b[i].reshape(1, -1),
                          ffn_w_in[i].astype(BF16), ffn_w_down[i].astype(BF16),
                          ffn_ln_g[i].reshape(1, -1), ffn_ln_b[i].reshape(1, -1))
    return x2.reshape(batch, seq, d)
```
